```python
import jax, jax.numpy as jnp
from jax import lax
import numpy as np

D_MODEL = 4096
BATCH = 2
SEQ = 4096
DEPTH = 2

HEAD_DIM = 128
D_LRU = D_MODEL // 2
LRU_HEADS = D_LRU // HEAD_DIM
D_DN = D_MODEL // 2
DN_HEADS = D_DN // HEAD_DIM
D_SC = D_MODEL
LRU_CONV_W = 4
DN_CONV_W = 4
SC_CONV_W = 3
LRU_C = 8.0
CHUNK = 64
EPS = 1e-6
N_EVEN = (DEPTH + 1) // 2
N_ODD = DEPTH // 2
D_IN0 = 2 * D_LRU + 4 * D_DN + 2 * DN_HEADS
D_IN1 = 4 * D_SC

kernel_name = "hybrid_rglru_gdn_shortconv"


def rmsnorm(x, w):
    xf = x.astype(jnp.float32)
    xf = xf * lax.rsqrt(jnp.mean(xf * xf, axis=-1, keepdims=True) + EPS)
    return xf.astype(x.dtype) * w


def l2norm(x):
    return x * lax.rsqrt(jnp.sum(x * x, axis=-1, keepdims=True) + EPS)


def causal_dwconv(x, w):
    K = w.shape[0]
    T = x.shape[1]
    xp = jnp.pad(x, ((0, 0), (K - 1, 0), (0, 0)))
    return sum(xp[:, k:k + T] * w[k] for k in range(K))


def rg_lru(x, w_r, b_r, w_i, b_i, lam):
    Bn, T, _ = x.shape
    xh = x.reshape(Bn, T, LRU_HEADS, HEAD_DIM)
    r = jax.nn.sigmoid(jnp.einsum('bthi,hij->bthj', xh, w_r).reshape(Bn, T, D_LRU) + b_r)
    gi = jax.nn.sigmoid(jnp.einsum('bthi,hij->bthj', xh, w_i).reshape(Bn, T, D_LRU) + b_i)
    log_a = -LRU_C * r.astype(jnp.float32) * jax.nn.softplus(-lam.astype(jnp.float32))
    a = jnp.exp(log_a)
    u = jnp.sqrt(-jnp.expm1(2.0 * log_a)) * (gi * x).astype(jnp.float32)

    def combine(c1, c2):
        a1, b1 = c1
        a2, b2 = c2
        return a1 * a2, a2 * b1 + b2

    _, h = lax.associative_scan(combine, (a, u), axis=1)
    return h.astype(x.dtype)


def gated_delta_chunked(q, k, v, g, beta):
    Bn, T, H, Dk = q.shape
    Dv = v.shape[-1]
    N = T // CHUNK
    f32 = jnp.float32
    q = l2norm(q.astype(f32)) * (Dk ** -0.5)
    k = l2norm(k.astype(f32))
    v = v.astype(f32)

    def to_chunks(t):
        return t.reshape(Bn, N, CHUNK, H, -1).transpose(0, 3, 1, 2, 4)

    q, k, v = to_chunks(q), to_chunks(k), to_chunks(v)
    g = g.astype(f32).reshape(Bn, N, CHUNK, H).transpose(0, 3, 1, 2)
    beta = beta.astype(f32).reshape(Bn, N, CHUNK, H).transpose(0, 3, 1, 2)
    g_cum = jnp.cumsum(g, axis=-1)
    causal = jnp.tril(jnp.ones((CHUNK, CHUNK), dtype=bool))
    strict = jnp.tril(jnp.ones((CHUNK, CHUNK), dtype=bool), -1)
    decay = jnp.exp(jnp.where(causal, g_cum[..., :, None] - g_cum[..., None, :], -jnp.inf))
    k_beta = k * beta[..., None]
    v_beta = v * beta[..., None]
    a_mat = jnp.where(strict, jnp.einsum('bhncd,bhnsd->bhncs', k_beta, k) * decay, 0.0)
    lhs = a_mat + jnp.eye(CHUNK, dtype=f32)
    rhs = jnp.concatenate([v_beta, k_beta * jnp.exp(g_cum)[..., None]], axis=-1)
    sol = lax.linalg.triangular_solve(lhs, rhs, left_side=True, lower=True, unit_diagonal=True)
    u_c, w_c = sol[..., :Dv], sol[..., Dv:]
    attn = jnp.einsum('bhncd,bhnsd->bhncs', q, k) * decay
    q_dec = q * jnp.exp(g_cum)[..., None]
    k_dec = k * jnp.exp(g_cum[..., -1:] - g_cum)[..., None]
    g_last = jnp.exp(g_cum[..., -1])

    xs = (u_c.transpose(2, 0, 1, 3, 4), w_c.transpose(2, 0, 1, 3, 4),
          attn.transpose(2, 0, 1, 3, 4), q_dec.transpose(2, 0, 1, 3, 4),
          k_dec.transpose(2, 0, 1, 3, 4), g_last.transpose(2, 0, 1))

    def step(S, inp):
        u_i, w_i, attn_i, qd_i, kd_i, gl_i = inp
        v_new = u_i - jnp.einsum('bhcd,bhde->bhce', w_i, S)
        o_i = jnp.einsum('bhcd,bhde->bhce', qd_i, S) + jnp.einsum('bhcs,bhse->bhce', attn_i, v_new)
        S = S * gl_i[..., None, None] + jnp.einsum('bhcd,bhce->bhde', kd_i, v_new)
        return S, o_i

    S0 = jnp.zeros((Bn, H, Dk, Dv), f32)
    _, o = lax.scan(step, S0, xs)
    return o.transpose(1, 0, 3, 2, 4).reshape(Bn, T, H, Dv)


def lru_deltanet_layer(x, norm_w, w_in, lru_conv_w, lru_conv_b, lru_w_r, lru_b_r,
                       lru_w_i, lru_b_i, lru_lambda, dn_conv_w, dn_a_log, dn_dt_bias,
                       dn_norm_w, w_out):
    Bn, T, _ = x.shape
    h = rmsnorm(x, norm_w)
    proj = h @ w_in
    splits = [D_LRU, 2 * D_LRU, 2 * D_LRU + 3 * D_DN, 2 * D_LRU + 4 * D_DN,
              2 * D_LRU + 4 * D_DN + DN_HEADS]
    xa, ga, qkv, z, b_raw, a_raw = jnp.split(proj, splits, axis=-1)
    xa = causal_dwconv(xa, lru_conv_w) + lru_conv_b
    ya = rg_lru(xa, lru_w_r, lru_b_r, lru_w_i, lru_b_i, lru_lambda) * jax.nn.silu(ga)
    qkv = jax.nn.silu(causal_dwconv(qkv, dn_conv_w))
    q, k, v = jnp.split(qkv, 3, axis=-1)
    q = q.reshape(Bn, T, DN_HEADS, HEAD_DIM)
    k = k.reshape(Bn, T, DN_HEADS, HEAD_DIM)
    v = v.reshape(Bn, T, DN_HEADS, HEAD_DIM)
    beta = jax.nn.sigmoid(b_raw.astype(jnp.float32))
    g = -jnp.exp(dn_a_log.astype(jnp.float32)) * jax.nn.softplus(
        a_raw.astype(jnp.float32) + dn_dt_bias.astype(jnp.float32))
    o = gated_delta_chunked(q, k, v, g, beta).astype(x.dtype)
    o = rmsnorm(o, dn_norm_w) * jax.nn.silu(z.reshape(Bn, T, DN_HEADS, HEAD_DIM))
    yb = o.reshape(Bn, T, D_DN)
    y = jnp.concatenate([ya, yb], axis=-1) @ w_out
    return x + y


def shortconv_layer(x, norm_w, w_in, conv_w, w_out):
    h = rmsnorm(x, norm_w)
    xin, gate_b, gate_c, z = jnp.split(h @ w_in, 4, axis=-1)
    y = gate_b * causal_dwconv(gate_c * xin, conv_w) * jax.nn.silu(z)
    return x + y @ w_out


def setup_inputs(seed: int = 0) -> dict:
    key = jax.random.key(seed)
    ks = jax.random.split(key, 24)
    f32 = jnp.float32
    nrm = lambda k, shape, s: jax.random.normal(k, shape, f32) * s
    x = jax.random.normal(ks[0], (BATCH, SEQ, D_MODEL), f32)
    a0 = jax.random.uniform(ks[8], (N_EVEN, D_LRU), f32, 0.9, 0.999)
    a_root = a0 ** (1.0 / LRU_C)
    lru_lambda = jnp.log(a_root) - jnp.log1p(-a_root)
    dn_a_log = jnp.log(jax.random.uniform(ks[10], (N_EVEN, DN_HEADS), f32, 1.0, 16.0))
    dt = jnp.exp(jax.random.uniform(ks[11], (N_EVEN, DN_HEADS), f32,
                                    float(np.log(1e-3)), float(np.log(1e-1))))
    dn_dt_bias = dt + jnp.log(-jnp.expm1(-dt))
    return {
        "x": x,
        "even_norm_w": 1.0 + nrm(ks[1], (N_EVEN, D_MODEL), 0.02),
        "even_w_in": nrm(ks[2], (N_EVEN, D_MODEL, D_IN0), D_MODEL ** -0.5),
        "lru_conv_w": nrm(ks[3], (N_EVEN, LRU_CONV_W, D_LRU), LRU_CONV_W ** -0.5),
        "lru_conv_b": nrm(ks[4], (N_EVEN, D_LRU), 0.02),
        "lru_w_r": nrm(ks[5], (N_EVEN, LRU_HEADS, HEAD_DIM, HEAD_DIM), HEAD_DIM ** -0.5),
        "lru_b_r": nrm(ks[6], (N_EVEN, D_LRU), 0.02),
        "lru_w_i": nrm(ks[7], (N_EVEN, LRU_HEADS, HEAD_DIM, HEAD_DIM), HEAD_DIM ** -0.5),
        "lru_b_i": nrm(ks[12], (N_EVEN, D_LRU), 0.02),
        "lru_lambda": lru_lambda,
        "dn_conv_w": nrm(ks[9], (N_EVEN, DN_CONV_W, 3 * D_DN), DN_CONV_W ** -0.5),
        "dn_a_log": dn_a_log,
        "dn_dt_bias": dn_dt_bias,
        "dn_norm_w": 1.0 + nrm(ks[13], (N_EVEN, HEAD_DIM), 0.02),
        "even_w_out": nrm(ks[14], (N_EVEN, D_LRU + D_DN, D_MODEL), (D_LRU + D_DN) ** -0.5),
        "odd_norm_w": 1.0 + nrm(ks[15], (N_ODD, D_MODEL), 0.02),
        "odd_w_in": nrm(ks[16], (N_ODD, D_MODEL, D_IN1), D_MODEL ** -0.5),
        "odd_conv_w": nrm(ks[17], (N_ODD, SC_CONV_W, D_SC), SC_CONV_W ** -0.5),
        "odd_w_out": nrm(ks[18], (N_ODD, D_SC, D_MODEL), D_SC ** -0.5),
        "final_norm_w": 1.0 + nrm(ks[19], (D_MODEL,), 0.02),
    }


def reference(x, even_norm_w, even_w_in, lru_conv_w, lru_conv_b, lru_w_r, lru_b_r,
              lru_w_i, lru_b_i, lru_lambda, dn_conv_w, dn_a_log, dn_dt_bias, dn_norm_w,
              even_w_out, odd_norm_w, odd_w_in, odd_conv_w, odd_w_out, final_norm_w):
    for layer in range(DEPTH):
        j = layer // 2
        if layer % 2 == 0:
            x = lru_deltanet_layer(x, even_norm_w[j], even_w_in[j], lru_conv_w[j], lru_conv_b[j],
                                   lru_w_r[j], lru_b_r[j], lru_w_i[j], lru_b_i[j], lru_lambda[j],
                                   dn_conv_w[j], dn_a_log[j], dn_dt_bias[j], dn_norm_w[j],
                                   even_w_out[j])
        else:
            x = shortconv_layer(x, odd_norm_w[j], odd_w_in[j], odd_conv_w[j], odd_w_out[j])
    return rmsnorm(x, final_norm_w)
```

```python
import functools

import jax
import jax.numpy as jnp
from jax import lax
from jax.experimental import pallas as pl
from jax.experimental.pallas import tpu as pltpu

HEAD_DIM = 128
CHUNK = 64
CONV_W = 4
SC_CONV_W = 3
LRU_C = 8.0
EPS = 1e-6
HIST = 8
VMEM_LIMIT = 56 * 1024 * 1024

F32 = jnp.float32
BF16 = jnp.bfloat16


def _params(sem, vmem=VMEM_LIMIT):
    return pltpu.CompilerParams(dimension_semantics=sem, vmem_limit_bytes=vmem)


def _sigmoid(x):
    return jax.nn.sigmoid(x)


def _softplus(x):
    return jnp.maximum(x, 0.0) + jnp.log1p(jnp.exp(-jnp.abs(x)))


def _dot(a, b, dims=(((1,), (0,)), ((), ()))):
    return lax.dot_general(a.astype(BF16), b.astype(BF16), dims, preferred_element_type=F32)


_NT = (((1,), (1,)), ((), ()))
_TN = (((0,), (0,)), ((), ()))


def _rmsnorm_kernel(x_ref, w_ref, o_ref):
    x = x_ref[...]
    ms = jnp.mean(x * x, axis=-1, keepdims=True)
    o_ref[...] = ((x * lax.rsqrt(ms + EPS)) * w_ref[...]).astype(o_ref.dtype)


def _rmsnorm(x2, w, out_dtype, tm=256):
    m, d = x2.shape
    tm = min(tm, m)
    return pl.pallas_call(
        _rmsnorm_kernel,
        grid=(m // tm,),
        in_specs=[pl.BlockSpec((tm, d), lambda i: (i, 0)),
                  pl.BlockSpec((1, d), lambda i: (0, 0))],
        out_specs=pl.BlockSpec((tm, d), lambda i: (i, 0)),
        out_shape=jax.ShapeDtypeStruct((m, d), out_dtype),
        compiler_params=_params(("arbitrary",)),
        name="rmsnorm",
    )(x2, w.reshape(1, d))


def _mm_kernel(*refs, n_a, has_res):
    a_refs = refs[:n_a]
    w_ref = refs[n_a]
    r_ref = refs[n_a + 1] if has_res else None
    o_ref, wbf_ref = refs[-2], refs[-1]

    @pl.when(pl.program_id(1) == 0)
    def _():
        wbf_ref[...] = w_ref[...].astype(BF16)

    acc = None
    off = 0
    for a_ref in a_refs:
        kk = a_ref.shape[1]
        part = jnp.dot(a_ref[...], wbf_ref[off:off + kk, :], preferred_element_type=F32)
        acc = part if acc is None else acc + part
        off += kk
    if has_res:
        acc = acc + r_ref[...]
    o_ref[...] = acc.astype(o_ref.dtype)


def _matmul(a_parts, w, n_cols, res=None, out_dtype=F32, tm=512, tn=512, name="matmul"):
    m = a_parts[0].shape[0]
    k = w.shape[0]
    assert sum(a.shape[1] for a in a_parts) == k
    tm = min(tm, m)
    tn = min(tn, n_cols)
    assert m % tm == 0 and n_cols % tn == 0
    in_specs = [pl.BlockSpec((tm, a.shape[1]), lambda j, i: (i, 0)) for a in a_parts]
    in_specs.append(pl.BlockSpec((k, tn), lambda j, i: (0, j)))
    args = list(a_parts) + [w]
    if res is not None:
        in_specs.append(pl.BlockSpec((tm, tn), lambda j, i: (i, j)))
        args.append(res)
    return pl.pallas_call(
        functools.partial(_mm_kernel, n_a=len(a_parts), has_res=res is not None),
        grid=(n_cols // tn, m // tm),
        in_specs=in_specs,
        out_specs=pl.BlockSpec((tm, tn), lambda j, i: (i, j)),
        out_shape=jax.ShapeDtypeStruct((m, n_cols), out_dtype),
        scratch_shapes=[pltpu.VMEM((k, tn), BF16)],
        compiler_params=_params(("arbitrary", "arbitrary")),
        name=name,
    )(*args)


def _causal_conv(x_raw, buf, cw, width, tt, first):
    @pl.when(first)
    def _():
        buf[0:HIST, :] = jnp.zeros((HIST, buf.shape[1]), F32)

    buf[HIST:HIST + tt, :] = x_raw
    y = None
    for kk in range(width - 1):
        start = HIST - (width - 1) + kk
        term = cw[kk:kk + 1, :] * buf[start:start + tt, :]
        y = term if y is None else y + term
    y = y + cw[width - 1:width, :] * x_raw
    buf[0:HIST, :] = x_raw[tt - HIST:tt, :]
    return y


def _rglru_kernel(xa_ref, ga_ref, cw_ref, cb_ref, wr_ref, br_ref, wi_ref, bi_ref, lam_ref,
                  o_ref, xbuf, hcar, *, tt, tc):
    t = pl.program_id(2)

    @pl.when(t == 0)
    def _():
        hcar[...] = jnp.zeros_like(hcar)

    xc = _causal_conv(xa_ref[...], xbuf, cw_ref[...], CONV_W, tt, t == 0) + cb_ref[...]

    xcb = xc.astype(BF16)
    r_parts, i_parts = [], []
    for hh in range(tc // HEAD_DIM):
        xh = xcb[:, hh * HEAD_DIM:(hh + 1) * HEAD_DIM]
        r_parts.append(jnp.dot(xh, wr_ref[hh].astype(BF16), preferred_element_type=F32))
        i_parts.append(jnp.dot(xh, wi_ref[hh].astype(BF16), preferred_element_type=F32))
    r = _sigmoid(jnp.concatenate(r_parts, axis=1) + br_ref[...])
    gi = _sigmoid(jnp.concatenate(i_parts, axis=1) + bi_ref[...])

    log_a = (-LRU_C) * r * _softplus(-lam_ref[...])
    a = jnp.exp(log_a)
    u = jnp.sqrt(-jnp.tanh(log_a) * (a * a + 1.0)) * (gi * xc)

    row = lax.broadcasted_iota(jnp.int32, (tt, tc), 0)
    s = 1
    while s < tt:
        keep = row >= s
        a_sh = jnp.where(keep, pltpu.roll(a, s, axis=0), 1.0)
        u_sh = jnp.where(keep, pltpu.roll(u, s, axis=0), 0.0)
        u = a * u_sh + u
        a = a * a_sh
        s *= 2
    h = a * hcar[...] + u
    hcar[...] = h[tt - 1:tt, :]

    ga = ga_ref[...]
    o_ref[...] = (h * (ga * _sigmoid(ga))).astype(o_ref.dtype)


def _rglru(p, bsz, seq, d_lru, cw, cb, w_r, b_r, w_i, b_i, lam, tt=256, tc=512):
    tt = min(tt, seq)
    tc = min(tc, d_lru)
    nt, nc = seq // tt, d_lru // tc
    hpb = tc // HEAD_DIM
    row = lambda v: v.reshape(1, d_lru)
    vec_spec = pl.BlockSpec((1, tc), lambda b, j, t: (0, j))
    return pl.pallas_call(
        functools.partial(_rglru_kernel, tt=tt, tc=tc),
        grid=(bsz, nc, nt),
        in_specs=[
            pl.BlockSpec((tt, tc), lambda b, j, t: (b * nt + t, j)),
            pl.BlockSpec((tt, tc), lambda b, j, t: (b * nt + t, nc + j)),
            pl.BlockSpec((CONV_W, tc), lambda b, j, t: (0, j)),
            vec_spec,
            pl.BlockSpec((hpb, HEAD_DIM, HEAD_DIM), lambda b, j, t: (j, 0, 0)),
            vec_spec,
            pl.BlockSpec((hpb, HEAD_DIM, HEAD_DIM), lambda b, j, t: (j, 0, 0)),
            vec_spec,
            vec_spec,
        ],
        out_specs=pl.BlockSpec((tt, tc), lambda b, j, t: (b * nt + t, j)),
        out_shape=jax.ShapeDtypeStruct((bsz * seq, d_lru), BF16),
        scratch_shapes=[pltpu.VMEM((HIST + tt, tc), F32), pltpu.VMEM((1, tc), F32)],
        compiler_params=_params(("arbitrary", "arbitrary", "arbitrary")),
        name="rglru",
    )(p, p, cw, row(cb), w_r, row(b_r), w_i, row(b_i), row(lam))


def _deltanet_kernel(alog_ref, dtb_ref, q_ref, k_ref, v_ref, z_ref, tail_ref,
                     cwq_ref, cwk_ref, cwv_ref, nw_ref, o_ref,
                     qbuf, kbuf, vbuf, s_ref, *, tt, nheads):
    hd = pl.program_id(1)
    t = pl.program_id(2)
    first = t == 0

    @pl.when(first)
    def _():
        s_ref[...] = jnp.zeros_like(s_ref)

    def conv_silu(ref, buf, cw_ref):
        y = _causal_conv(ref[...], buf, cw_ref[...], CONV_W, tt, first)
        return y * _sigmoid(y)

    q = conv_silu(q_ref, qbuf, cwq_ref)
    k = conv_silu(k_ref, kbuf, cwk_ref)
    v = conv_silu(v_ref, vbuf, cwv_ref)
    q = (q * lax.rsqrt(jnp.sum(q * q, axis=-1, keepdims=True) + EPS)) * (HEAD_DIM ** -0.5)
    k = k * lax.rsqrt(jnp.sum(k * k, axis=-1, keepdims=True) + EPS)

    tail = tail_ref[...]
    lane = lax.broadcasted_iota(jnp.int32, tail.shape, 1)
    b_raw = jnp.sum(jnp.where(lane == hd, tail, 0.0), axis=-1, keepdims=True)
    a_raw = jnp.sum(jnp.where(lane == hd + nheads, tail, 0.0), axis=-1, keepdims=True)
    beta = jnp.broadcast_to(_sigmoid(b_raw), (tt, HEAD_DIM))
    neg_a = -jnp.exp(jnp.full((1, 1), alog_ref[hd], F32))
    g = neg_a * _softplus(a_raw + jnp.full((1, 1), dtb_ref[hd], F32))
    gc = jnp.broadcast_to(g, (tt, HEAD_DIM))

    rmod = lax.broadcasted_iota(jnp.int32, (tt, HEAD_DIM), 0) & (CHUNK - 1)
    s = 1
    while s < CHUNK:
        gc = gc + jnp.where(rmod >= s, pltpu.roll(gc, s, axis=0), 0.0)
        s *= 2
    egc = jnp.exp(gc)

    ii = lax.broadcasted_iota(jnp.int32, (CHUNK, CHUNK), 0)
    jj = lax.broadcasted_iota(jnp.int32, (CHUNK, CHUNK), 1)
    causal = ii >= jj
    strict = ii > jj
    nw = nw_ref[...]

    for c in range(tt // CHUNK):
        lo = c * CHUNK
        qc, kc, vc = q[lo:lo + CHUNK], k[lo:lo + CHUNK], v[lo:lo + CHUNK]
        bc, gcc, egc_c = beta[lo:lo + CHUNK], gc[lo:lo + CHUNK], egc[lo:lo + CHUNK]
        g_last = gcc[CHUNK - 1:CHUNK, :]
        if CHUNK == HEAD_DIM // 2:
            pair = lo // HEAD_DIM * HEAD_DIM
            g_t = gc[pair:pair + HEAD_DIM, :].T
            gc_row = g_t[0:CHUNK, lo - pair:lo - pair + CHUNK]
        else:
            raise NotImplementedError
        decay = jnp.exp(jnp.where(causal, gcc[:, 0:CHUNK] - gc_row, -jnp.inf))

        kb = kc * bc
        vb = vc * bc
        n_mat = -jnp.where(strict, _dot(kb, kc, _NT) * decay, 0.0)
        x_sol = jnp.concatenate([vb, kb * egc_c], axis=1)
        x_sol = x_sol + _dot(n_mat, x_sol)
        p_mat = n_mat
        pw = 2
        while pw < CHUNK:
            p_mat = _dot(p_mat, p_mat)
            x_sol = x_sol + _dot(p_mat, x_sol)
            pw *= 2
        u_c = x_sol[:, 0:HEAD_DIM]
        w_c = x_sol[:, HEAD_DIM:2 * HEAD_DIM]

        attn = _dot(qc, kc, _NT) * decay
        q_dec = qc * egc_c
        k_dec = kc * jnp.exp(g_last - gcc)

        s_mat = s_ref[...]
        v_new = u_c - _dot(w_c, s_mat)
        o_c = _dot(q_dec, s_mat) + _dot(attn, v_new)
        s_ref[...] = s_mat * jnp.exp(g_last) + _dot(k_dec, v_new, _TN)

        ms = jnp.mean(o_c * o_c, axis=-1, keepdims=True)
        zc = z_ref[lo:lo + CHUNK, :]
        y = ((o_c * lax.rsqrt(ms + EPS)) * nw) * (zc * _sigmoid(zc))
        o_ref[lo:lo + CHUNK, :] = y.astype(o_ref.dtype)


def _deltanet(p, tail, bsz, seq, d_lru, d_dn, conv_w, a_log, dt_bias, norm_w, tt=256):
    nheads = d_dn // HEAD_DIM
    tt = min(tt, seq)
    nt = seq // tt
    base = 2 * d_lru // HEAD_DIM

    def col_spec(off):
        return pl.BlockSpec((tt, HEAD_DIM), lambda b, h, t: (b * nt + t, off + h))

    def cw_spec(off):
        return pl.BlockSpec((CONV_W, HEAD_DIM), lambda b, h, t: (0, off + h))

    smem = pl.BlockSpec(memory_space=pltpu.SMEM)
    return pl.pallas_call(
        functools.partial(_deltanet_kernel, tt=tt, nheads=nheads),
        grid=(bsz, nheads, nt),
        in_specs=[
            smem, smem,
            col_spec(base), col_spec(base + nheads), col_spec(base + 2 * nheads),
            col_spec(base + 3 * nheads),
            pl.BlockSpec((tt, 2 * nheads), lambda b, h, t: (b * nt + t, 0)),
            cw_spec(0), cw_spec(nheads), cw_spec(2 * nheads),
            pl.BlockSpec((1, HEAD_DIM), lambda b, h, t: (0, 0)),
        ],
        out_specs=pl.BlockSpec((tt, HEAD_DIM), lambda b, h, t: (b * nt + t, h)),
        out_shape=jax.ShapeDtypeStruct((bsz * seq, d_dn), BF16),
        scratch_shapes=[pltpu.VMEM((HIST + tt, HEAD_DIM), F32)] * 3
        + [pltpu.VMEM((HEAD_DIM, HEAD_DIM), F32)],
        compiler_params=_params(("arbitrary", "arbitrary", "arbitrary")),
        name="deltanet",
    )(a_log, dt_bias, p, p, p, p, tail, conv_w, conv_w, conv_w, norm_w.reshape(1, HEAD_DIM))


def _shortconv_kernel(a_ref, wx_ref, wb_ref, wc_ref, wz_ref, cw_ref, o_ref,
                      wbf_ref, ubuf, *, tm, blocks_per_seq):
    i = pl.program_id(1)

    @pl.when(i == 0)
    def _():
        for n, w_ref in enumerate((wx_ref, wb_ref, wc_ref, wz_ref)):
            wbf_ref[n] = w_ref[...].astype(BF16)

    a = a_ref[...]
    xin = jnp.dot(a, wbf_ref[0], preferred_element_type=F32)
    gate_b = jnp.dot(a, wbf_ref[1], preferred_element_type=F32)
    gate_c = jnp.dot(a, wbf_ref[2], preferred_element_type=F32)
    z = jnp.dot(a, wbf_ref[3], preferred_element_type=F32)
    conv = _causal_conv(gate_c * xin, ubuf, cw_ref[...], SC_CONV_W, tm, i % blocks_per_seq == 0)
    o_ref[...] = ((gate_b * conv) * (z * _sigmoid(z))).astype(o_ref.dtype)


def _shortconv(h, w_in, conv_w, seq, d_sc, tm=512, tn=256):
    m, k = h.shape
    tm = min(tm, seq)
    tn = min(tn, d_sc)
    nb = d_sc // tn

    def w_spec(part):
        return pl.BlockSpec((k, tn), lambda j, i: (0, part * nb + j))

    return pl.pallas_call(
        functools.partial(_shortconv_kernel, tm=tm, blocks_per_seq=seq // tm),
        grid=(nb, m // tm),
        in_specs=[pl.BlockSpec((tm, k), lambda j, i: (i, 0)),
                  w_spec(0), w_spec(1), w_spec(2), w_spec(3),
                  pl.BlockSpec((SC_CONV_W, tn), lambda j, i: (0, j))],
        out_specs=pl.BlockSpec((tm, tn), lambda j, i: (i, j)),
        out_shape=jax.ShapeDtypeStruct((m, d_sc), BF16),
        scratch_shapes=[pltpu.VMEM((4, k, tn), BF16), pltpu.VMEM((HIST + tm, tn), F32)],
        compiler_params=_params(("arbitrary", "arbitrary")),
        name="shortconv",
    )(h, w_in, w_in, w_in, w_in, conv_w)


def _lru_deltanet_layer(x2, bsz, seq, norm_w, w_in, lru_conv_w, lru_conv_b, lru_w_r, lru_b_r,
                        lru_w_i, lru_b_i, lru_lambda, dn_conv_w, dn_a_log, dn_dt_bias,
                        dn_norm_w, w_out):
    d_lru = lru_lambda.shape[0]
    d_dn = dn_conv_w.shape[1] // 3
    n_main = 2 * d_lru + 4 * d_dn
    h = _rmsnorm(x2, norm_w, BF16)
    p = _matmul([h], w_in, n_main, name="in_proj0")
    tail = _matmul([h], w_in[:, n_main:], w_in.shape[1] - n_main, name="in_proj0_tail")
    ya = _rglru(p, bsz, seq, d_lru, lru_conv_w, lru_conv_b, lru_w_r, lru_b_r, lru_w_i, lru_b_i,
                lru_lambda)
    yb = _deltanet(p, tail, bsz, seq, d_lru, d_dn, dn_conv_w, dn_a_log, dn_dt_bias, dn_norm_w)
    return _matmul([ya, yb], w_out, w_out.shape[1], res=x2, name="out_proj0")


def _shortconv_layer(x2, seq, norm_w, w_in, conv_w, w_out):
    h = _rmsnorm(x2, norm_w, BF16)
    y = _shortconv(h, w_in, conv_w, seq, conv_w.shape[1])
    return _matmul([y], w_out, w_out.shape[1], res=x2, name="out_proj1")


def kernel(x, even_norm_w, even_w_in, lru_conv_w, lru_conv_b, lru_w_r, lru_b_r, lru_w_i, lru_b_i,
           lru_lambda, dn_conv_w, dn_a_log, dn_dt_bias, dn_norm_w, even_w_out, odd_norm_w,
           odd_w_in, odd_conv_w, odd_w_out, final_norm_w):
    bsz, seq, d = x.shape
    x2 = x.reshape(bsz * seq, d)
    depth = even_norm_w.shape[0] + odd_norm_w.shape[0]
    for layer in range(depth):
        j = layer // 2
        if layer % 2 == 0:
            x2 = _lru_deltanet_layer(
                x2, bsz, seq, even_norm_w[j], even_w_in[j], lru_conv_w[j], lru_conv_b[j],
                lru_w_r[j], lru_b_r[j], lru_w_i[j], lru_b_i[j], lru_lambda[j], dn_conv_w[j],
                dn_a_log[j], dn_dt_bias[j], dn_norm_w[j], even_w_out[j])
        else:
            x2 = _shortconv_layer(x2, seq, odd_norm_w[j], odd_w_in[j], odd_conv_w[j], odd_w_out[j])
    return _rmsnorm(x2, final_norm_w, x.dtype).reshape(bsz, seq, d)
```

```python
import functools

import jax
import jax.numpy as jnp
from jax import lax
from jax.experimental import pallas as pl
from jax.experimental.pallas import tpu as pltpu

HEAD_DIM = 128
CHUNK = 64
GROUP = 2 * CHUNK
LANES = 128
CONV_W = 4
SC_CONV_W = 3
LRU_C = 8.0
EPS = 1e-6
HIST = 8
VMEM_LIMIT = 56 * 1024 * 1024

F32 = jnp.float32
BF16 = jnp.bfloat16

assert GROUP == HEAD_DIM == LANES


def _params(sem, vmem=VMEM_LIMIT):
    return pltpu.CompilerParams(dimension_semantics=sem, vmem_limit_bytes=vmem)


def _sigmoid(x):
    return jax.nn.sigmoid(x)


def _softplus(x):
    return jnp.maximum(x, 0.0) + jnp.log1p(jnp.exp(-jnp.abs(x)))


def _dot(a, b, dims=(((1,), (0,)), ((), ()))):
    return lax.dot_general(a.astype(BF16), b.astype(BF16), dims, preferred_element_type=F32)


_NT = (((1,), (1,)), ((), ()))


def _rmsnorm_kernel(x_ref, w_ref, o_ref):
    x = x_ref[...]
    ms = jnp.mean(x * x, axis=-1, keepdims=True)
    o_ref[...] = ((x * lax.rsqrt(ms + EPS)) * w_ref[...]).astype(o_ref.dtype)


def _rmsnorm(x2, w, out_dtype, tm=256):
    m, d = x2.shape
    tm = min(tm, m)
    return pl.pallas_call(
        _rmsnorm_kernel,
        grid=(m // tm,),
        in_specs=[pl.BlockSpec((tm, d), lambda i: (i, 0)),
                  pl.BlockSpec((1, d), lambda i: (0, 0))],
        out_specs=pl.BlockSpec((tm, d), lambda i: (i, 0)),
        out_shape=jax.ShapeDtypeStruct((m, d), out_dtype),
        compiler_params=_params(("arbitrary",)),
        name="rmsnorm",
    )(x2, w.reshape(1, d))


def _mm_kernel(*refs, n_a, has_res):
    a_refs = refs[:n_a]
    w_ref = refs[n_a]
    r_ref = refs[n_a + 1] if has_res else None
    o_ref, wbf_ref = refs[-2], refs[-1]

    @pl.when(pl.program_id(1) == 0)
    def _():
        wbf_ref[...] = w_ref[...].astype(BF16)

    acc = None
    off = 0
    for a_ref in a_refs:
        kk = a_ref.shape[1]
        part = jnp.dot(a_ref[...], wbf_ref[off:off + kk, :], preferred_element_type=F32)
        acc = part if acc is None else acc + part
        off += kk
    if has_res:
        acc = acc + r_ref[...]
    o_ref[...] = acc.astype(o_ref.dtype)


def _matmul(a_parts, w3, layer, n_cols, res=None, out_dtype=F32, tm=1024, tn=512, name="matmul"):
    m = a_parts[0].shape[0]
    k = w3.shape[1]
    assert sum(a.shape[1] for a in a_parts) == k
    tm = min(tm, m)
    tn = min(tn, n_cols)
    assert m % tm == 0 and n_cols % tn == 0
    in_specs = [pl.BlockSpec((tm, a.shape[1]), lambda j, i: (i, 0)) for a in a_parts]
    in_specs.append(pl.BlockSpec((None, k, tn), lambda j, i: (layer, 0, j)))
    args = list(a_parts) + [w3]
    if res is not None:
        in_specs.append(pl.BlockSpec((tm, tn), lambda j, i: (i, j)))
        args.append(res)
    return pl.pallas_call(
        functools.partial(_mm_kernel, n_a=len(a_parts), has_res=res is not None),
        grid=(n_cols // tn, m // tm),
        in_specs=in_specs,
        out_specs=pl.BlockSpec((tm, tn), lambda j, i: (i, j)),
        out_shape=jax.ShapeDtypeStruct((m, n_cols), out_dtype),
        scratch_shapes=[pltpu.VMEM((k, tn), BF16)],
        compiler_params=_params(("arbitrary", "arbitrary")),
        name=name,
    )(*args)


def _causal_conv(x_raw, buf, cw, width, tt, first):
    @pl.when(first)
    def _():
        buf[0:HIST, :] = jnp.zeros((HIST, buf.shape[1]), F32)

    buf[HIST:HIST + tt, :] = x_raw
    y = None
    for kk in range(width - 1):
        start = HIST - (width - 1) + kk
        term = cw[kk:kk + 1, :] * buf[start:start + tt, :]
        y = term if y is None else y + term
    y = y + cw[width - 1:width, :] * x_raw
    buf[0:HIST, :] = x_raw[tt - HIST:tt, :]
    return y


def _rglru_kernel(xa_ref, ga_ref, cw_ref, cb_ref, wr_ref, br_ref, wi_ref, bi_ref, lam_ref,
                  o_ref, xbuf, hcar, *, tt, tc):
    t = pl.program_id(2)

    @pl.when(t == 0)
    def _():
        hcar[...] = jnp.zeros_like(hcar)

    xc = _causal_conv(xa_ref[...], xbuf, cw_ref[...], CONV_W, tt, t == 0) + cb_ref[...]

    xcb = xc.astype(BF16)
    r_parts, i_parts = [], []
    for hh in range(tc // HEAD_DIM):
        xh = xcb[:, hh * HEAD_DIM:(hh + 1) * HEAD_DIM]
        r_parts.append(jnp.dot(xh, wr_ref[hh].astype(BF16), preferred_element_type=F32))
        i_parts.append(jnp.dot(xh, wi_ref[hh].astype(BF16), preferred_element_type=F32))
    r = _sigmoid(jnp.concatenate(r_parts, axis=1) + br_ref[...])
    gi = _sigmoid(jnp.concatenate(i_parts, axis=1) + bi_ref[...])

    log_a = (-LRU_C) * r * _softplus(-lam_ref[...])
    a = jnp.exp(log_a)
    u = jnp.sqrt(-jnp.tanh(log_a) * (a * a + 1.0)) * (gi * xc)

    row = lax.broadcasted_iota(jnp.int32, (tt, tc), 0)
    s = 1
    while s < tt:
        keep = row >= s
        a_sh = jnp.where(keep, pltpu.roll(a, s, axis=0), 1.0)
        u_sh = jnp.where(keep, pltpu.roll(u, s, axis=0), 0.0)
        u = a * u_sh + u
        a = a * a_sh
        s *= 2
    h = a * hcar[...] + u
    hcar[...] = h[tt - 1:tt, :]

    ga = ga_ref[...]
    o_ref[...] = (h * (ga * _sigmoid(ga))).astype(o_ref.dtype)


def _rglru(p, bsz, seq, d_lru, cw, cb, w_r, b_r, w_i, b_i, lam, tt=256, tc=512):
    tt = min(tt, seq)
    tc = min(tc, d_lru)
    nt, nc = seq // tt, d_lru // tc
    hpb = tc // HEAD_DIM
    row = lambda v: v.reshape(1, d_lru)
    vec_spec = pl.BlockSpec((1, tc), lambda b, j, t: (0, j))
    return pl.pallas_call(
        functools.partial(_rglru_kernel, tt=tt, tc=tc),
        grid=(bsz, nc, nt),
        in_specs=[
            pl.BlockSpec((tt, tc), lambda b, j, t: (b * nt + t, j)),
            pl.BlockSpec((tt, tc), lambda b, j, t: (b * nt + t, nc + j)),
            pl.BlockSpec((CONV_W, tc), lambda b, j, t: (0, j)),
            vec_spec,
            pl.BlockSpec((hpb, HEAD_DIM, HEAD_DIM), lambda b, j, t: (j, 0, 0)),
            vec_spec,
            pl.BlockSpec((hpb, HEAD_DIM, HEAD_DIM), lambda b, j, t: (j, 0, 0)),
            vec_spec,
            vec_spec,
        ],
        out_specs=pl.BlockSpec((tt, tc), lambda b, j, t: (b * nt + t, j)),
        out_shape=jax.ShapeDtypeStruct((bsz * seq, d_lru), BF16),
        scratch_shapes=[pltpu.VMEM((HIST + tt, tc), F32), pltpu.VMEM((1, tc), F32)],
        compiler_params=_params(("arbitrary", "arbitrary", "arbitrary")),
        name="rglru",
    )(p, p, cw, row(cb), w_r, row(b_r), w_i, row(b_i), row(lam))


def _dn_gates_kernel(tail_ref, alog_ref, dtb_ref, o_ref, *, nheads, tt):
    tail = tail_ref[...]
    lane = lax.broadcasted_iota(jnp.int32, (tt, LANES), 1)
    is_a = (lane >= nheads) & (lane < 2 * nheads)
    g = jnp.where(is_a, -jnp.exp(alog_ref[...]) * _softplus(tail + dtb_ref[...]), 0.0)
    rmod = lax.broadcasted_iota(jnp.int32, (tt, LANES), 0) & (CHUNK - 1)
    s = 1
    while s < CHUNK:
        g = g + jnp.where(rmod >= s, pltpu.roll(g, s, axis=0), 0.0)
        s *= 2
    o_ref[...] = jnp.where(is_a, g, _sigmoid(tail))


def _dn_gates(tail, a_log, dt_bias, nheads, tt=512):
    m = tail.shape[0]
    tt = min(tt, m)
    pad_row = lambda v: jnp.pad(v, (nheads, LANES - 2 * nheads)).reshape(1, LANES)
    return pl.pallas_call(
        functools.partial(_dn_gates_kernel, nheads=nheads, tt=tt),
        grid=(m // tt,),
        in_specs=[pl.BlockSpec((tt, LANES), lambda i: (i, 0)),
                  pl.BlockSpec((1, LANES), lambda i: (0, 0)),
                  pl.BlockSpec((1, LANES), lambda i: (0, 0))],
        out_specs=pl.BlockSpec((tt, LANES), lambda i: (i, 0)),
        out_shape=jax.ShapeDtypeStruct((m, LANES), F32),
        compiler_params=_params(("arbitrary",)),
        name="dn_gates",
    )(tail, pad_row(a_log), pad_row(dt_bias))


def _dn_prep_kernel(q_ref, k_ref, v_ref, gate_ref, cwq_ref, cwk_ref, cwv_ref,
                    u_ref, wq_ref, kdt_ref, attn_ref, gl_ref,
                    qbuf, kbuf, vbuf, *, tt, hg, nheads):
    hp = pl.program_id(1)
    first = pl.program_id(2) == 0

    def conv_silu(ref, buf, cw_ref):
        y = _causal_conv(ref[...], buf, cw_ref[...], CONV_W, tt, first)
        return y * _sigmoid(y)

    q_all = conv_silu(q_ref, qbuf, cwq_ref)
    k_all = conv_silu(k_ref, kbuf, cwk_ref)
    v_all = conv_silu(v_ref, vbuf, cwv_ref)

    gate = gate_ref[...]
    lane = lax.broadcasted_iota(jnp.int32, (tt, LANES), 1)
    ii = lax.broadcasted_iota(jnp.int32, (GROUP, GROUP), 0)
    jj = lax.broadcasted_iota(jnp.int32, (GROUP, GROUP), 1)
    same_chunk = (jnp.bitwise_xor(ii, jj) & CHUNK) == 0
    causal = same_chunk & (ii >= jj)
    strict = same_chunk & (ii > jj)
    eye = jnp.where(ii == jj, 1.0, 0.0)
    top = ii < CHUNK

    chains = []
    for hh in range(hg):
        hd = hp * hg + hh
        cs = slice(hh * HEAD_DIM, (hh + 1) * HEAD_DIM)
        q = q_all[:, cs]
        k = k_all[:, cs]
        v = v_all[:, cs]
        q = (q * lax.rsqrt(jnp.sum(q * q, axis=-1, keepdims=True) + EPS)) * (HEAD_DIM ** -0.5)
        k = k * lax.rsqrt(jnp.sum(k * k, axis=-1, keepdims=True) + EPS)
        beta = jnp.broadcast_to(
            jnp.sum(jnp.where(lane == hd, gate, 0.0), axis=-1, keepdims=True), (tt, LANES))
        gc = jnp.broadcast_to(
            jnp.sum(jnp.where(lane == hd + nheads, gate, 0.0), axis=-1, keepdims=True), (tt, LANES))
        egc = jnp.exp(gc)
        for g in range(tt // GROUP):
            rs = slice(g * GROUP, (g + 1) * GROUP)
            chains.append(dict(cs=cs, g=g, rs=rs, q=q[rs], k=k[rs], v=v[rs], beta=beta[rs],
                               gc=gc[rs], egc=egc[rs]))

    for ch in chains:
        gcg = ch["gc"]
        ch["decay"] = jnp.exp(jnp.where(causal, gcg - gcg.T, -jnp.inf))
        ch["kb"] = ch["k"] * ch["beta"]
    for ch in chains:
        ch["kk"] = _dot(ch["kb"], ch["k"], _NT)
    for ch in chains:
        ch["qk"] = _dot(ch["q"], ch["k"], _NT)
    for ch in chains:
        ch["p"] = -jnp.where(strict, ch["kk"] * ch["decay"], 0.0)
        ch["t"] = eye + ch["p"]
    pw = 2
    while pw < CHUNK:
        for ch in chains:
            ch["p"] = _dot(ch["p"], ch["p"])
        for ch in chains:
            ch["t"] = ch["t"] + _dot(ch["p"], ch["t"])
        pw *= 2
    for ch in chains:
        rhs = jnp.concatenate([ch["v"] * ch["beta"], ch["kb"] * ch["egc"]], axis=1)
        ch["sol"] = _dot(ch["t"], rhs)

    for ch in chains:
        cs, g, rs, gcg = ch["cs"], ch["g"], ch["rs"], ch["gc"]
        lo = g * GROUP
        u = ch["sol"][:, 0:HEAD_DIM]
        w = ch["sol"][:, HEAD_DIM:2 * HEAD_DIM].astype(BF16)
        attn = ch["qk"] * ch["decay"]
        q_dec = (ch["q"] * ch["egc"]).astype(BF16)
        gl0 = gcg[CHUNK - 1:CHUNK, :]
        gl1 = gcg[GROUP - 1:GROUP, :]
        k_dec = ch["k"] * jnp.exp(jnp.where(top, gl0, gl1) - gcg)

        u_ref[rs, cs] = u
        for c in range(2):
            cr = slice(c * CHUNK, (c + 1) * CHUNK)
            wq_ref[2 * lo + c * GROUP:2 * lo + c * GROUP + CHUNK, cs] = w[cr]
            wq_ref[2 * lo + c * GROUP + CHUNK:2 * lo + (c + 1) * GROUP, cs] = q_dec[cr]
        kdt_ref[rs, cs] = k_dec.T.astype(BF16)
        attn_ref[rs, cs] = attn.astype(BF16)
        gl_ref[2 * g:2 * g + 1, cs] = jnp.exp(gl0)
        gl_ref[2 * g + 1:2 * g + 2, cs] = jnp.exp(gl1)


def _dn_scan_kernel(u_ref, wq_ref, kdt_ref, attn_ref, gl_ref, z_ref, nw_ref, o_ref, s_ref,
                    *, tt, hg):
    @pl.when(pl.program_id(2) == 0)
    def _():
        s_ref[...] = jnp.zeros_like(s_ref)

    nw = nw_ref[...]
    zeros = jnp.zeros((CHUNK, HEAD_DIM), F32)
    cols = [slice(hh * HEAD_DIM, (hh + 1) * HEAD_DIM) for hh in range(hg)]
    s_mats = [s_ref[hh] for hh in range(hg)]
    for g in range(tt // GROUP):
        lo = g * GROUP
        rs = slice(lo, lo + GROUP)
        v_prev = [zeros] * hg
        outs = [[] for _ in range(hg)]
        for c in range(2):
            r = [_dot(wq_ref[2 * lo + c * GROUP:2 * lo + (c + 1) * GROUP, cs], s_mats[hh])
                 for hh, cs in enumerate(cols)]
            v_new = [u_ref[lo + c * CHUNK:lo + (c + 1) * CHUNK, cs] - r[hh][0:CHUNK]
                     for hh, cs in enumerate(cols)]
            for hh, cs in enumerate(cols):
                v_upd = jnp.concatenate([zeros, v_new[hh]] if c else [v_new[hh], zeros], axis=0)
                s_mats[hh] = (s_mats[hh] * gl_ref[2 * g + c:2 * g + c + 1, cs]
                              + _dot(kdt_ref[rs, cs], v_upd))
            for hh, cs in enumerate(cols):
                v_att = jnp.concatenate([v_prev[hh], v_new[hh]] if c else [v_new[hh], zeros], axis=0)
                outs[hh].append(r[hh][CHUNK:GROUP]
                                + _dot(attn_ref[lo + c * CHUNK:lo + (c + 1) * CHUNK, cs], v_att))
            v_prev = v_new
        for hh, cs in enumerate(cols):
            o = jnp.concatenate(outs[hh], axis=0)
            ms = jnp.mean(o * o, axis=-1, keepdims=True)
            z = z_ref[rs, cs]
            o_ref[rs, cs] = (((o * lax.rsqrt(ms + EPS)) * nw) * (z * _sigmoid(z))).astype(o_ref.dtype)
    for hh in range(hg):
        s_ref[hh] = s_mats[hh]


def _deltanet(p, gates, bsz, seq, d_lru, d_dn, conv_w3, layer, norm_w, tt=512, hg_prep=2,
              hg_scan=8):
    nheads = d_dn // HEAD_DIM
    m = bsz * seq
    tt = min(tt, seq)
    hg_prep = min(hg_prep, nheads)
    hg_scan = min(hg_scan, nheads)
    nt = seq // tt
    base = 2 * d_lru // HEAD_DIM
    ncb = 2 * tt // GROUP

    def prep_cols(off):
        return pl.BlockSpec((tt, hg_prep * HEAD_DIM),
                            lambda b, h, t: (b * nt + t, (base + off * nheads) // hg_prep + h))

    def prep_cw(off):
        return pl.BlockSpec((None, CONV_W, hg_prep * HEAD_DIM),
                            lambda b, h, t: (layer, 0, off * nheads // hg_prep + h))

    assert base % hg_prep == 0 and nheads % hg_prep == 0 and nheads % hg_scan == 0
    wide = hg_prep * HEAD_DIM
    row_blk = lambda rows: pl.BlockSpec((rows, wide), lambda b, h, t: (b * nt + t, h))
    u, wq, kdt, attn, gl = pl.pallas_call(
        functools.partial(_dn_prep_kernel, tt=tt, hg=hg_prep, nheads=nheads),
        grid=(bsz, nheads // hg_prep, nt),
        in_specs=[prep_cols(0), prep_cols(1), prep_cols(2),
                  pl.BlockSpec((tt, LANES), lambda b, h, t: (b * nt + t, 0)),
                  prep_cw(0), prep_cw(1), prep_cw(2)],
        out_specs=[row_blk(tt), row_blk(2 * tt), row_blk(tt), row_blk(tt), row_blk(ncb)],
        out_shape=[jax.ShapeDtypeStruct((m, d_dn), F32),
                   jax.ShapeDtypeStruct((2 * m, d_dn), BF16),
                   jax.ShapeDtypeStruct((m, d_dn), BF16),
                   jax.ShapeDtypeStruct((m, d_dn), BF16),
                   jax.ShapeDtypeStruct((m // CHUNK, d_dn), F32)],
        scratch_shapes=[pltpu.VMEM((HIST + tt, wide), F32)] * 3,
        compiler_params=_params(("arbitrary", "arbitrary", "arbitrary")),
        name="dn_prep",
    )(p, p, p, gates, conv_w3, conv_w3, conv_w3)

    wide = hg_scan * HEAD_DIM
    row_blk = lambda rows: pl.BlockSpec((rows, wide), lambda b, h, t: (b * nt + t, h))
    z_off = (base + 3 * nheads) // hg_scan
    assert (base + 3 * nheads) % hg_scan == 0
    return pl.pallas_call(
        functools.partial(_dn_scan_kernel, tt=tt, hg=hg_scan),
        grid=(bsz, nheads // hg_scan, nt),
        in_specs=[row_blk(tt), row_blk(2 * tt), row_blk(tt), row_blk(tt), row_blk(ncb),
                  pl.BlockSpec((tt, wide), lambda b, h, t: (b * nt + t, z_off + h)),
                  pl.BlockSpec((1, HEAD_DIM), lambda b, h, t: (0, 0))],
        out_specs=row_blk(tt),
        out_shape=jax.ShapeDtypeStruct((m, d_dn), BF16),
        scratch_shapes=[pltpu.VMEM((hg_scan, HEAD_DIM, HEAD_DIM), F32)],
        compiler_params=_params(("arbitrary", "arbitrary", "arbitrary")),
        name="dn_scan",
    )(u, wq, kdt, attn, gl, p, norm_w.reshape(1, HEAD_DIM))


def _shortconv_kernel(a_ref, wx_ref, wb_ref, wc_ref, wz_ref, cw_ref, o_ref,
                      wbf_ref, ubuf, *, tm, blocks_per_seq):
    i = pl.program_id(1)

    @pl.when(i == 0)
    def _():
        for n, w_ref in enumerate((wx_ref, wb_ref, wc_ref, wz_ref)):
            wbf_ref[n] = w_ref[...].astype(BF16)

    a = a_ref[...]
    xin = jnp.dot(a, wbf_ref[0], preferred_element_type=F32)
    gate_b = jnp.dot(a, wbf_ref[1], preferred_element_type=F32)
    gate_c = jnp.dot(a, wbf_ref[2], preferred_element_type=F32)
    z = jnp.dot(a, wbf_ref[3], preferred_element_type=F32)
    conv = _causal_conv(gate_c * xin, ubuf, cw_ref[...], SC_CONV_W, tm, i % blocks_per_seq == 0)
    o_ref[...] = ((gate_b * conv) * (z * _sigmoid(z))).astype(o_ref.dtype)


def _shortconv(h, w_in3, conv_w3, layer, seq, tm=512, tn=256):
    m, k = h.shape
    d_sc = conv_w3.shape[2]
    tm = min(tm, seq)
    tn = min(tn, d_sc)
    nb = d_sc // tn

    def w_spec(part):
        return pl.BlockSpec((None, k, tn), lambda j, i: (layer, 0, part * nb + j))

    return pl.pallas_call(
        functools.partial(_shortconv_kernel, tm=tm, blocks_per_seq=seq // tm),
        grid=(nb, m // tm),
        in_specs=[pl.BlockSpec((tm, k), lambda j, i: (i, 0)),
                  w_spec(0), w_spec(1), w_spec(2), w_spec(3),
                  pl.BlockSpec((None, SC_CONV_W, tn), lambda j, i: (layer, 0, j))],
        out_specs=pl.BlockSpec((tm, tn), lambda j, i: (i, j)),
        out_shape=jax.ShapeDtypeStruct((m, d_sc), BF16),
        scratch_shapes=[pltpu.VMEM((4, k, tn), BF16), pltpu.VMEM((HIST + tm, tn), F32)],
        compiler_params=_params(("arbitrary", "arbitrary")),
        name="shortconv",
    )(h, w_in3, w_in3, w_in3, w_in3, conv_w3)


def kernel(x, even_norm_w, even_w_in, lru_conv_w, lru_conv_b, lru_w_r, lru_b_r, lru_w_i, lru_b_i,
           lru_lambda, dn_conv_w, dn_a_log, dn_dt_bias, dn_norm_w, even_w_out, odd_norm_w,
           odd_w_in, odd_conv_w, odd_w_out, final_norm_w):
    bsz, seq, d = x.shape
    x2 = x.reshape(bsz * seq, d)
    d_lru = lru_lambda.shape[1]
    d_dn = dn_conv_w.shape[2] // 3
    nheads = d_dn // HEAD_DIM
    n_main = 2 * d_lru + 4 * d_dn
    depth = even_norm_w.shape[0] + odd_norm_w.shape[0]
    for layer in range(depth):
        j = layer // 2
        if layer % 2 == 0:
            h = _rmsnorm(x2, even_norm_w[j], BF16)
            p = _matmul([h], even_w_in, j, n_main, name="in_proj0")
            w_tail = jnp.pad(even_w_in[j, :, n_main:], ((0, 0), (0, LANES - 2 * nheads)))[None]
            tail = _matmul([h], w_tail, 0, LANES, name="in_proj0_tail")
            gates = _dn_gates(tail, dn_a_log[j], dn_dt_bias[j], nheads)
            ya = _rglru(p, bsz, seq, d_lru, lru_conv_w[j], lru_conv_b[j], lru_w_r[j], lru_b_r[j],
                        lru_w_i[j], lru_b_i[j], lru_lambda[j])
            yb = _deltanet(p, gates, bsz, seq, d_lru, d_dn, dn_conv_w, j, dn_norm_w[j])
            x2 = _matmul([ya, yb], even_w_out, j, d, res=x2, name="out_proj0")
        else:
            h = _rmsnorm(x2, odd_norm_w[j], BF16)
            y = _shortconv(h, odd_w_in, odd_conv_w, j, seq)
            x2 = _matmul([y], odd_w_out, j, d, res=x2, name="out_proj1")
    return _rmsnorm(x2, final_norm_w, x.dtype).reshape(bsz, seq, d)
```

```python
import functools

import jax
import jax.numpy as jnp
from jax import lax
from jax.experimental import pallas as pl
from jax.experimental.pallas import tpu as pltpu

HEAD_DIM = 128
CHUNK = 64
GROUP = 2 * CHUNK
LANES = 128
SUBLANES = 8
CONV_W = 4
SC_CONV_W = 3
LRU_C = 8.0
EPS = 1e-6
HIST = 8
VMEM_LIMIT = 56 * 1024 * 1024
VMEM_SLACK = 4 * 1024 * 1024
RES_TN = 512

F32 = jnp.float32
BF16 = jnp.bfloat16

assert GROUP == HEAD_DIM == LANES


def _params(sem, vmem=VMEM_LIMIT):
    return pltpu.CompilerParams(dimension_semantics=sem, vmem_limit_bytes=vmem)


def _sigmoid(x):
    return jax.nn.sigmoid(x)


def _softplus(x):
    return jnp.maximum(x, 0.0) + jnp.log1p(jnp.exp(-jnp.abs(x)))


def _dot(a, b, dims=(((1,), (0,)), ((), ()))):
    return lax.dot_general(a.astype(BF16), b.astype(BF16), dims, preferred_element_type=F32)


_NT = (((1,), (1,)), ((), ()))


def _rmsnorm_kernel(x_ref, w_ref, o_ref):
    x = x_ref[...]
    ms = jnp.mean(x * x, axis=-1, keepdims=True)
    o_ref[...] = ((x * lax.rsqrt(ms + EPS)) * w_ref[...]).astype(o_ref.dtype)


def _rmsnorm(x2, w, out_dtype, tm=256):
    m, d = x2.shape
    tm = min(tm, m)
    return pl.pallas_call(
        _rmsnorm_kernel,
        grid=(m // tm,),
        in_specs=[pl.BlockSpec((tm, d), lambda i: (i, 0)),
                  pl.BlockSpec((1, d), lambda i: (0, 0))],
        out_specs=pl.BlockSpec((tm, d), lambda i: (i, 0)),
        out_shape=jax.ShapeDtypeStruct((m, d), out_dtype),
        compiler_params=_params(("arbitrary",)),
        name="rmsnorm",
    )(x2, w.reshape(1, d))


def _mm_kernel(*refs, n_a, has_res, w_is_nk):
    a_refs = refs[:n_a]
    w_ref = refs[n_a]
    r_ref = refs[n_a + 1] if has_res else None
    o_ref, wbf_ref = refs[-2], refs[-1]

    @pl.when(pl.program_id(1) == 0)
    def _():
        wbf_ref[...] = w_ref[...].astype(BF16)

    acc = None
    off = 0
    for a_ref in a_refs:
        kk = a_ref.shape[1]
        if w_is_nk:
            part = lax.dot_general(a_ref[...], wbf_ref[:, off:off + kk], _NT,
                                   preferred_element_type=F32)
        else:
            part = jnp.dot(a_ref[...], wbf_ref[off:off + kk, :], preferred_element_type=F32)
        acc = part if acc is None else acc + part
        off += kk
    if has_res:
        acc = acc + r_ref[...]
    o_ref[...] = acc.astype(o_ref.dtype)


def _matmul(a_parts, w3, layer, n_cols, res=None, out_dtype=F32, tm=1024, tn=1024,
            w_is_nk=False, name="matmul"):
    m = a_parts[0].shape[0]
    k = w3.shape[2] if w_is_nk else w3.shape[1]
    assert sum(a.shape[1] for a in a_parts) == k
    tm = min(tm, m)
    tn = max(t for t in range(LANES, min(tn, n_cols) + 1, LANES) if n_cols % t == 0)
    assert m % tm == 0
    in_specs = [pl.BlockSpec((tm, a.shape[1]), lambda j, i: (i, 0)) for a in a_parts]
    out_bytes = tm * tn * jnp.dtype(out_dtype).itemsize
    tiles = 2 * (tm * k * 2 + out_bytes + (tm * tn * 4 if res is not None else 0)) + k * tn * 2
    w_mode = pl.Buffered(1) if tiles + 2 * k * tn * 4 > VMEM_LIMIT - VMEM_SLACK else None
    if w_is_nk:
        in_specs.append(pl.BlockSpec((None, tn, k), lambda j, i: (layer, j, 0),
                                     pipeline_mode=w_mode))
    else:
        in_specs.append(pl.BlockSpec((None, k, tn), lambda j, i: (layer, 0, j),
                                     pipeline_mode=w_mode))
    args = list(a_parts) + [w3]
    if res is not None:
        in_specs.append(pl.BlockSpec((tm, tn), lambda j, i: (i, j)))
        args.append(res)
    return pl.pallas_call(
        functools.partial(_mm_kernel, n_a=len(a_parts), has_res=res is not None, w_is_nk=w_is_nk),
        grid=(n_cols // tn, m // tm),
        in_specs=in_specs,
        out_specs=pl.BlockSpec((tm, tn), lambda j, i: (i, j)),
        out_shape=jax.ShapeDtypeStruct((m, n_cols), out_dtype),
        scratch_shapes=[pltpu.VMEM((tn, k) if w_is_nk else (k, tn), BF16)],
        compiler_params=_params(("arbitrary", "arbitrary")),
        name=name,
    )(*args)


def _causal_conv(x_raw, hist, cw, width, tt, first):
    @pl.when(first)
    def _():
        hist[...] = jnp.zeros_like(hist)

    ext = jnp.concatenate([hist[...], x_raw], axis=0)
    hist[...] = x_raw[tt - HIST:tt, :]
    z = cw[0:1, :] * ext
    for kk in range(1, width):
        z = cw[kk:kk + 1, :] * ext + pltpu.roll(z, 1, axis=0)
    return z[HIST:HIST + tt, :]


def _rglru_kernel(xa_ref, ga_ref, cw_ref, cb_ref, wr_ref, br_ref, wi_ref, bi_ref, lam_ref,
                  o_ref, xbuf, hcar, *, tt, tc):
    t = pl.program_id(2)

    @pl.when(t == 0)
    def _():
        hcar[...] = jnp.zeros_like(hcar)

    xc = _causal_conv(xa_ref[...], xbuf, cw_ref[...], CONV_W, tt, t == 0) + cb_ref[...]

    xcb = xc.astype(BF16)
    r_parts, i_parts = [], []
    for hh in range(tc // HEAD_DIM):
        xh = xcb[:, hh * HEAD_DIM:(hh + 1) * HEAD_DIM]
        r_parts.append(jnp.dot(xh, wr_ref[hh].astype(BF16), preferred_element_type=F32))
        i_parts.append(jnp.dot(xh, wi_ref[hh].astype(BF16), preferred_element_type=F32))
    r = _sigmoid(jnp.concatenate(r_parts, axis=1) + br_ref[...])
    gi = _sigmoid(jnp.concatenate(i_parts, axis=1) + bi_ref[...])

    log_a = (-LRU_C) * r * _softplus(-lam_ref[...])
    a = jnp.exp(log_a)
    u = jnp.sqrt(-jnp.tanh(log_a) * (a * a + 1.0)) * (gi * xc)

    rin = lax.broadcasted_iota(jnp.int32, (tt, tc), 0) & (SUBLANES - 1)
    s = 1
    while s < SUBLANES:
        keep = rin >= s
        a_sh = jnp.where(keep, pltpu.roll(a, s, axis=0), 1.0)
        u_sh = jnp.where(keep, pltpu.roll(u, s, axis=0), 0.0)
        u = a * u_sh + u
        a = a * a_sh
        s *= 2
    carry = hcar[...]
    tiles = []
    for i in range(tt // SUBLANES):
        rows = slice(i * SUBLANES, (i + 1) * SUBLANES)
        h_i = a[rows] * carry + u[rows]
        carry = h_i[SUBLANES - 1:SUBLANES, :]
        tiles.append(h_i)
    h = jnp.concatenate(tiles, axis=0)
    hcar[...] = carry

    ga = ga_ref[...]
    o_ref[...] = (h * (ga * _sigmoid(ga))).astype(o_ref.dtype)


def _rglru(p, bsz, seq, d_lru, cw, cb, w_r, b_r, w_i, b_i, lam, tt=256, tc=512):
    tt = min(tt, seq)
    tc = min(tc, d_lru)
    nt, nc = seq // tt, d_lru // tc
    hpb = tc // HEAD_DIM
    row = lambda v: v.reshape(1, d_lru)
    vec_spec = pl.BlockSpec((1, tc), lambda b, j, t: (0, j))
    return pl.pallas_call(
        functools.partial(_rglru_kernel, tt=tt, tc=tc),
        grid=(bsz, nc, nt),
        in_specs=[
            pl.BlockSpec((tt, tc), lambda b, j, t: (b * nt + t, j)),
            pl.BlockSpec((tt, tc), lambda b, j, t: (b * nt + t, nc + j)),
            pl.BlockSpec((CONV_W, tc), lambda b, j, t: (0, j)),
            vec_spec,
            pl.BlockSpec((hpb, HEAD_DIM, HEAD_DIM), lambda b, j, t: (j, 0, 0)),
            vec_spec,
            pl.BlockSpec((hpb, HEAD_DIM, HEAD_DIM), lambda b, j, t: (j, 0, 0)),
            vec_spec,
            vec_spec,
        ],
        out_specs=pl.BlockSpec((tt, tc), lambda b, j, t: (b * nt + t, j)),
        out_shape=jax.ShapeDtypeStruct((bsz * seq, d_lru), BF16),
        scratch_shapes=[pltpu.VMEM((HIST, tc), F32), pltpu.VMEM((1, tc), F32)],
        compiler_params=_params(("arbitrary", "arbitrary", "arbitrary")),
        name="rglru",
    )(p, p, cw, row(cb), w_r, row(b_r), w_i, row(b_i), row(lam))


def _dn_gates_kernel(tail_ref, alog_ref, dtb_ref, o_ref, *, nheads, tt):
    tail = tail_ref[...]
    lane = lax.broadcasted_iota(jnp.int32, (tt, LANES), 1)
    is_a = (lane >= nheads) & (lane < 2 * nheads)
    g = jnp.where(is_a, -jnp.exp(alog_ref[...]) * _softplus(tail + dtb_ref[...]), 0.0)
    rmod = lax.broadcasted_iota(jnp.int32, (tt, LANES), 0) & (CHUNK - 1)
    s = 1
    while s < CHUNK:
        g = g + jnp.where(rmod >= s, pltpu.roll(g, s, axis=0), 0.0)
        s *= 2
    o_ref[...] = jnp.where(is_a, g, _sigmoid(tail))


def _dn_gates(tail, a_log, dt_bias, nheads, tt=512):
    m = tail.shape[0]
    tt = min(tt, m)
    pad_row = lambda v: jnp.pad(v, (nheads, LANES - 2 * nheads)).reshape(1, LANES)
    return pl.pallas_call(
        functools.partial(_dn_gates_kernel, nheads=nheads, tt=tt),
        grid=(m // tt,),
        in_specs=[pl.BlockSpec((tt, LANES), lambda i: (i, 0)),
                  pl.BlockSpec((1, LANES), lambda i: (0, 0)),
                  pl.BlockSpec((1, LANES), lambda i: (0, 0))],
        out_specs=pl.BlockSpec((tt, LANES), lambda i: (i, 0)),
        out_shape=jax.ShapeDtypeStruct((m, LANES), F32),
        compiler_params=_params(("arbitrary",)),
        name="dn_gates",
    )(tail, pad_row(a_log), pad_row(dt_bias))


def _dn_prep_kernel(q_ref, k_ref, v_ref, gate_ref, cwq_ref, cwk_ref, cwv_ref,
                    u_ref, wq_ref, kdt_ref, attn_ref, gl_ref,
                    qbuf, kbuf, vbuf, *, tt, hg, nheads):
    hp = pl.program_id(1)
    first = pl.program_id(2) == 0

    def conv_silu(ref, buf, cw_ref):
        y = _causal_conv(ref[...], buf, cw_ref[...], CONV_W, tt, first)
        return y * _sigmoid(y)

    q_all = conv_silu(q_ref, qbuf, cwq_ref)
    k_all = conv_silu(k_ref, kbuf, cwk_ref)
    v_all = conv_silu(v_ref, vbuf, cwv_ref)

    gate = gate_ref[...]
    lane = lax.broadcasted_iota(jnp.int32, (tt, LANES), 1)
    ii = lax.broadcasted_iota(jnp.int32, (GROUP, GROUP), 0)
    jj = lax.broadcasted_iota(jnp.int32, (GROUP, GROUP), 1)
    same_chunk = (jnp.bitwise_xor(ii, jj) & CHUNK) == 0
    causal = same_chunk & (ii >= jj)
    strict = same_chunk & (ii > jj)
    eye = jnp.where(ii == jj, 1.0, 0.0)
    top = ii < CHUNK

    chains = []
    for hh in range(hg):
        hd = hp * hg + hh
        cs = slice(hh * HEAD_DIM, (hh + 1) * HEAD_DIM)
        q = q_all[:, cs]
        k = k_all[:, cs]
        v = v_all[:, cs]
        q = (q * lax.rsqrt(jnp.sum(q * q, axis=-1, keepdims=True) + EPS)) * (HEAD_DIM ** -0.5)
        k = k * lax.rsqrt(jnp.sum(k * k, axis=-1, keepdims=True) + EPS)
        beta = jnp.broadcast_to(
            jnp.sum(jnp.where(lane == hd, gate, 0.0), axis=-1, keepdims=True), (tt, LANES))
        gc = jnp.broadcast_to(
            jnp.sum(jnp.where(lane == hd + nheads, gate, 0.0), axis=-1, keepdims=True), (tt, LANES))
        egc = jnp.exp(gc)
        for g in range(tt // GROUP):
            rs = slice(g * GROUP, (g + 1) * GROUP)
            chains.append(dict(cs=cs, g=g, rs=rs, q=q[rs], k=k[rs], v=v[rs], beta=beta[rs],
                               gc=gc[rs], egc=egc[rs]))

    for ch in chains:
        gcg = ch["gc"]
        ch["decay"] = jnp.exp(jnp.where(causal, gcg - gcg.T, -jnp.inf))
        ch["kb"] = ch["k"] * ch["beta"]
    for ch in chains:
        ch["kk"] = _dot(ch["kb"], ch["k"], _NT)
    for ch in chains:
        ch["qk"] = _dot(ch["q"], ch["k"], _NT)
    for ch in chains:
        ch["p"] = -jnp.where(strict, ch["kk"] * ch["decay"], 0.0)
        ch["t"] = eye + ch["p"]
    pw = 2
    while pw < CHUNK:
        for ch in chains:
            ch["p"] = _dot(ch["p"], ch["p"])
        for ch in chains:
            ch["t"] = ch["t"] + _dot(ch["p"], ch["t"])
        pw *= 2
    for ch in chains:
        rhs = jnp.concatenate([ch["v"] * ch["beta"], ch["kb"] * ch["egc"]], axis=1)
        ch["sol"] = _dot(ch["t"], rhs)

    for ch in chains:
        cs, g, rs, gcg = ch["cs"], ch["g"], ch["rs"], ch["gc"]
        lo = g * GROUP
        u = ch["sol"][:, 0:HEAD_DIM]
        w = ch["sol"][:, HEAD_DIM:2 * HEAD_DIM].astype(BF16)
        attn = ch["qk"] * ch["decay"]
        q_dec = (ch["q"] * ch["egc"]).astype(BF16)
        gl0 = gcg[CHUNK - 1:CHUNK, :]
        gl1 = gcg[GROUP - 1:GROUP, :]
        k_dec = ch["k"] * jnp.exp(jnp.where(top, gl0, gl1) - gcg)

        u_ref[rs, cs] = u
        for c in range(2):
            cr = slice(c * CHUNK, (c + 1) * CHUNK)
            wq_ref[2 * lo + c * GROUP:2 * lo + c * GROUP + CHUNK, cs] = w[cr]
            wq_ref[2 * lo + c * GROUP + CHUNK:2 * lo + (c + 1) * GROUP, cs] = q_dec[cr]
        kdt_ref[rs, cs] = k_dec.T.astype(BF16)
        attn_ref[rs, cs] = attn.astype(BF16)
        gl_ref[2 * g:2 * g + 1, cs] = jnp.exp(gl0)
        gl_ref[2 * g + 1:2 * g + 2, cs] = jnp.exp(gl1)


def _dn_scan_kernel(u_ref, wq_ref, kdt_ref, attn_ref, gl_ref, z_ref, nw_ref, o_ref, s_ref,
                    *, tt, hg):
    @pl.when(pl.program_id(2) == 0)
    def _():
        s_ref[...] = jnp.zeros_like(s_ref)

    nw = nw_ref[...]
    zeros = jnp.zeros((CHUNK, HEAD_DIM), F32)
    cols = [slice(hh * HEAD_DIM, (hh + 1) * HEAD_DIM) for hh in range(hg)]
    s_mats = [s_ref[hh] for hh in range(hg)]
    for g in range(tt // GROUP):
        lo = g * GROUP
        rs = slice(lo, lo + GROUP)
        v_prev = [zeros] * hg
        outs = [[] for _ in range(hg)]
        for c in range(2):
            r = [_dot(wq_ref[2 * lo + c * GROUP:2 * lo + (c + 1) * GROUP, cs], s_mats[hh])
                 for hh, cs in enumerate(cols)]
            v_new = [u_ref[lo + c * CHUNK:lo + (c + 1) * CHUNK, cs] - r[hh][0:CHUNK]
                     for hh, cs in enumerate(cols)]
            for hh, cs in enumerate(cols):
                v_upd = jnp.concatenate([zeros, v_new[hh]] if c else [v_new[hh], zeros], axis=0)
                s_mats[hh] = (s_mats[hh] * gl_ref[2 * g + c:2 * g + c + 1, cs]
                              + _dot(kdt_ref[rs, cs], v_upd))
            for hh, cs in enumerate(cols):
                v_att = jnp.concatenate([v_prev[hh], v_new[hh]] if c else [v_new[hh], zeros], axis=0)
                outs[hh].append(r[hh][CHUNK:GROUP]
                                + _dot(attn_ref[lo + c * CHUNK:lo + (c + 1) * CHUNK, cs], v_att))
            v_prev = v_new
        for hh, cs in enumerate(cols):
            o = jnp.concatenate(outs[hh], axis=0)
            ms = jnp.mean(o * o, axis=-1, keepdims=True)
            z = z_ref[rs, cs]
            o_ref[rs, cs] = (((o * lax.rsqrt(ms + EPS)) * nw) * (z * _sigmoid(z))).astype(o_ref.dtype)
    for hh in range(hg):
        s_ref[hh] = s_mats[hh]


def _deltanet(p, gates, bsz, seq, d_lru, d_dn, conv_w3, layer, norm_w, tt=512, hg_prep=2,
              hg_scan=8):
    nheads = d_dn // HEAD_DIM
    m = bsz * seq
    tt = min(tt, seq)
    hg_prep = min(hg_prep, nheads)
    hg_scan = min(hg_scan, nheads)
    nt = seq // tt
    base = 2 * d_lru // HEAD_DIM
    ncb = 2 * tt // GROUP

    def prep_cols(off):
        return pl.BlockSpec((tt, hg_prep * HEAD_DIM),
                            lambda b, h, t: (b * nt + t, (base + off * nheads) // hg_prep + h))

    def prep_cw(off):
        return pl.BlockSpec((None, CONV_W, hg_prep * HEAD_DIM),
                            lambda b, h, t: (layer, 0, off * nheads // hg_prep + h))

    assert base % hg_prep == 0 and nheads % hg_prep == 0 and nheads % hg_scan == 0
    wide = hg_prep * HEAD_DIM
    row_blk = lambda rows: pl.BlockSpec((rows, wide), lambda b, h, t: (b * nt + t, h))
    u, wq, kdt, attn, gl = pl.pallas_call(
        functools.partial(_dn_prep_kernel, tt=tt, hg=hg_prep, nheads=nheads),
        grid=(bsz, nheads // hg_prep, nt),
        in_specs=[prep_cols(0), prep_cols(1), prep_cols(2),
                  pl.BlockSpec((tt, LANES), lambda b, h, t: (b * nt + t, 0)),
                  prep_cw(0), prep_cw(1), prep_cw(2)],
        out_specs=[row_blk(tt), row_blk(2 * tt), row_blk(tt), row_blk(tt), row_blk(ncb)],
        out_shape=[jax.ShapeDtypeStruct((m, d_dn), F32),
                   jax.ShapeDtypeStruct((2 * m, d_dn), BF16),
                   jax.ShapeDtypeStruct((m, d_dn), BF16),
                   jax.ShapeDtypeStruct((m, d_dn), BF16),
                   jax.ShapeDtypeStruct((m // CHUNK, d_dn), F32)],
        scratch_shapes=[pltpu.VMEM((HIST, wide), F32)] * 3,
        compiler_params=_params(("arbitrary", "arbitrary", "arbitrary")),
        name="dn_prep",
    )(p, p, p, gates, conv_w3, conv_w3, conv_w3)

    wide = hg_scan * HEAD_DIM
    row_blk = lambda rows: pl.BlockSpec((rows, wide), lambda b, h, t: (b * nt + t, h))
    z_off = (base + 3 * nheads) // hg_scan
    assert (base + 3 * nheads) % hg_scan == 0
    return pl.pallas_call(
        functools.partial(_dn_scan_kernel, tt=tt, hg=hg_scan),
        grid=(bsz, nheads // hg_scan, nt),
        in_specs=[row_blk(tt), row_blk(2 * tt), row_blk(tt), row_blk(tt), row_blk(ncb),
                  pl.BlockSpec((tt, wide), lambda b, h, t: (b * nt + t, z_off + h)),
                  pl.BlockSpec((1, HEAD_DIM), lambda b, h, t: (0, 0))],
        out_specs=row_blk(tt),
        out_shape=jax.ShapeDtypeStruct((m, d_dn), BF16),
        scratch_shapes=[pltpu.VMEM((hg_scan, HEAD_DIM, HEAD_DIM), F32)],
        compiler_params=_params(("arbitrary", "arbitrary", "arbitrary")),
        name="dn_scan",
    )(u, wq, kdt, attn, gl, p, norm_w.reshape(1, HEAD_DIM))


def _shortconv_kernel(a_ref, wx_ref, wb_ref, wc_ref, wz_ref, cw_ref, o_ref,
                      wbf_ref, ubuf, *, tm, blocks_per_seq):
    i = pl.program_id(1)

    @pl.when(i == 0)
    def _():
        for n, w_ref in enumerate((wx_ref, wb_ref, wc_ref, wz_ref)):
            wbf_ref[n] = w_ref[...].astype(BF16)

    a = a_ref[...]
    xin = jnp.dot(a, wbf_ref[0], preferred_element_type=F32)
    gate_b = jnp.dot(a, wbf_ref[1], preferred_element_type=F32)
    gate_c = jnp.dot(a, wbf_ref[2], preferred_element_type=F32)
    z = jnp.dot(a, wbf_ref[3], preferred_element_type=F32)
    conv = _causal_conv(gate_c * xin, ubuf, cw_ref[...], SC_CONV_W, tm, i % blocks_per_seq == 0)
    o_ref[...] = ((gate_b * conv) * (z * _sigmoid(z))).astype(o_ref.dtype)


def _shortconv(h, w_in3, conv_w3, layer, seq, tm=512, tn=256):
    m, k = h.shape
    d_sc = conv_w3.shape[2]
    tm = min(tm, seq)
    tn = min(tn, d_sc)
    nb = d_sc // tn

    def w_spec(part):
        return pl.BlockSpec((None, k, tn), lambda j, i: (layer, 0, part * nb + j))

    return pl.pallas_call(
        functools.partial(_shortconv_kernel, tm=tm, blocks_per_seq=seq // tm),
        grid=(nb, m // tm),
        in_specs=[pl.BlockSpec((tm, k), lambda j, i: (i, 0)),
                  w_spec(0), w_spec(1), w_spec(2), w_spec(3),
                  pl.BlockSpec((None, SC_CONV_W, tn), lambda j, i: (layer, 0, j))],
        out_specs=pl.BlockSpec((tm, tn), lambda j, i: (i, j)),
        out_shape=jax.ShapeDtypeStruct((m, d_sc), BF16),
        scratch_shapes=[pltpu.VMEM((4, k, tn), BF16), pltpu.VMEM((HIST, tn), F32)],
        compiler_params=_params(("arbitrary", "arbitrary")),
        name="shortconv",
    )(h, w_in3, w_in3, w_in3, w_in3, conv_w3)


def kernel(x, even_norm_w, even_w_in, lru_conv_w, lru_conv_b, lru_w_r, lru_b_r, lru_w_i, lru_b_i,
           lru_lambda, dn_conv_w, dn_a_log, dn_dt_bias, dn_norm_w, even_w_out, odd_norm_w,
           odd_w_in, odd_conv_w, odd_w_out, final_norm_w):
    bsz, seq, d = x.shape
    x2 = x.reshape(bsz * seq, d)
    d_lru = lru_lambda.shape[1]
    d_dn = dn_conv_w.shape[2] // 3
    nheads = d_dn // HEAD_DIM
    n_main = 2 * d_lru + 4 * d_dn
    depth = even_norm_w.shape[0] + odd_norm_w.shape[0]
    for layer in range(depth):
        j = layer // 2
        if layer % 2 == 0:
            h = _rmsnorm(x2, even_norm_w[j], BF16)
            w_in_nk = jnp.swapaxes(even_w_in, 1, 2)
            p = _matmul([h], w_in_nk, j, n_main, w_is_nk=True, name="in_proj0")
            w_tail = jnp.pad(w_in_nk[j, n_main:, :], ((0, LANES - 2 * nheads), (0, 0)))[None]
            tail = _matmul([h], w_tail, 0, LANES, w_is_nk=True, name="in_proj0_tail")
            gates = _dn_gates(tail, dn_a_log[j], dn_dt_bias[j], nheads)
            ya = _rglru(p, bsz, seq, d_lru, lru_conv_w[j], lru_conv_b[j], lru_w_r[j], lru_b_r[j],
                        lru_w_i[j], lru_b_i[j], lru_lambda[j])
            yb = _deltanet(p, gates, bsz, seq, d_lru, d_dn, dn_conv_w, j, dn_norm_w[j])
            x2 = _matmul([ya, yb], even_w_out, j, d, res=x2, tn=RES_TN, name="out_proj0")
        else:
            h = _rmsnorm(x2, odd_norm_w[j], BF16)
            y = _shortconv(h, odd_w_in, odd_conv_w, j, seq)
            x2 = _matmul([y], odd_w_out, j, d, res=x2, tn=RES_TN, name="out_proj1")
    return _rmsnorm(x2, final_norm_w, x.dtype).reshape(bsz, seq, d)
```

```python
import functools

import jax
import jax.numpy as jnp
from jax import lax
from jax.experimental import pallas as pl
from jax.experimental.pallas import tpu as pltpu

HEAD_DIM = 128
CHUNK = 64
GROUP = 2 * CHUNK
LANES = 128
SUBLANES = 8
CONV_W = 4
SC_CONV_W = 3
LRU_C = 8.0
EPS = 1e-6
HIST = 8
VMEM_LIMIT = 56 * 1024 * 1024
SC_TN = 256
VMEM_SLACK = 4 * 1024 * 1024
RES_TN = 512

F32 = jnp.float32
BF16 = jnp.bfloat16

assert GROUP == HEAD_DIM == LANES


def _params(sem, vmem=VMEM_LIMIT):
    return pltpu.CompilerParams(dimension_semantics=sem, vmem_limit_bytes=vmem)


def _sigmoid(x):
    return jax.nn.sigmoid(x)


def _softplus(x):
    return jnp.maximum(x, 0.0) + jnp.log1p(jnp.exp(-jnp.abs(x)))


def _dot(a, b, dims=(((1,), (0,)), ((), ()))):
    return lax.dot_general(a.astype(BF16), b.astype(BF16), dims, preferred_element_type=F32)


_NT = (((1,), (1,)), ((), ()))


def _rmsnorm_kernel(x_ref, w_ref, o_ref):
    x = x_ref[...]
    ms = jnp.mean(x * x, axis=-1, keepdims=True)
    o_ref[...] = ((x * lax.rsqrt(ms + EPS)) * w_ref[...]).astype(o_ref.dtype)


def _rmsnorm(x2, w, out_dtype, tm=256):
    m, d = x2.shape
    tm = min(tm, m)
    return pl.pallas_call(
        _rmsnorm_kernel,
        grid=(m // tm,),
        in_specs=[pl.BlockSpec((tm, d), lambda i: (i, 0)),
                  pl.BlockSpec((1, d), lambda i: (0, 0))],
        out_specs=pl.BlockSpec((tm, d), lambda i: (i, 0)),
        out_shape=jax.ShapeDtypeStruct((m, d), out_dtype),
        compiler_params=_params(("arbitrary",)),
        name="rmsnorm",
    )(x2, w.reshape(1, d))


def _mm_kernel(*refs, n_a, has_res, w_is_nk):
    a_refs = refs[:n_a]
    w_ref = refs[n_a]
    r_ref = refs[n_a + 1] if has_res else None
    o_ref, wbf_ref = refs[-2], refs[-1]

    @pl.when(pl.program_id(1) == 0)
    def _():
        wbf_ref[...] = w_ref[...].astype(BF16)

    acc = None
    off = 0
    for a_ref in a_refs:
        kk = a_ref.shape[1]
        if w_is_nk:
            part = lax.dot_general(a_ref[...], wbf_ref[:, off:off + kk], _NT,
                                   preferred_element_type=F32)
        else:
            part = jnp.dot(a_ref[...], wbf_ref[off:off + kk, :], preferred_element_type=F32)
        acc = part if acc is None else acc + part
        off += kk
    if has_res:
        acc = acc + r_ref[...]
    o_ref[...] = acc.astype(o_ref.dtype)


def _matmul(a_parts, w3, layer, n_cols, res=None, out_dtype=F32, tm=1024, tn=1024,
            w_is_nk=False, name="matmul"):
    m = a_parts[0].shape[0]
    k = w3.shape[2] if w_is_nk else w3.shape[1]
    assert sum(a.shape[1] for a in a_parts) == k
    tm = min(tm, m)
    tn = max(t for t in range(LANES, min(tn, n_cols) + 1, LANES) if n_cols % t == 0)
    assert m % tm == 0
    in_specs = [pl.BlockSpec((tm, a.shape[1]), lambda j, i: (i, 0)) for a in a_parts]
    out_bytes = tm * tn * jnp.dtype(out_dtype).itemsize
    tiles = 2 * (tm * k * 2 + out_bytes + (tm * tn * 4 if res is not None else 0)) + k * tn * 2
    w_mode = pl.Buffered(1) if tiles + 2 * k * tn * 4 > VMEM_LIMIT - VMEM_SLACK else None
    if w_is_nk:
        in_specs.append(pl.BlockSpec((None, tn, k), lambda j, i: (layer, j, 0),
                                     pipeline_mode=w_mode))
    else:
        in_specs.append(pl.BlockSpec((None, k, tn), lambda j, i: (layer, 0, j),
                                     pipeline_mode=w_mode))
    args = list(a_parts) + [w3]
    if res is not None:
        in_specs.append(pl.BlockSpec((tm, tn), lambda j, i: (i, j)))
        args.append(res)
    return pl.pallas_call(
        functools.partial(_mm_kernel, n_a=len(a_parts), has_res=res is not None, w_is_nk=w_is_nk),
        grid=(n_cols // tn, m // tm),
        in_specs=in_specs,
        out_specs=pl.BlockSpec((tm, tn), lambda j, i: (i, j)),
        out_shape=jax.ShapeDtypeStruct((m, n_cols), out_dtype),
        scratch_shapes=[pltpu.VMEM((tn, k) if w_is_nk else (k, tn), BF16)],
        compiler_params=_params(("arbitrary", "arbitrary")),
        name=name,
    )(*args)


def _causal_conv(x_raw, hist, cw, width, tt, first):
    @pl.when(first)
    def _():
        hist[...] = jnp.zeros_like(hist)

    ext = jnp.concatenate([hist[...], x_raw], axis=0)
    hist[...] = x_raw[tt - HIST:tt, :]
    z = cw[0:1, :] * ext
    for kk in range(1, width):
        z = cw[kk:kk + 1, :] * ext + pltpu.roll(z, 1, axis=0)
    return z[HIST:HIST + tt, :]


def _rglru_kernel(xa_ref, ga_ref, cw_ref, cb_ref, wr_ref, br_ref, wi_ref, bi_ref, lam_ref,
                  o_ref, xbuf, hcar, *, tt, tc):
    t = pl.program_id(2)

    @pl.when(t == 0)
    def _():
        hcar[...] = jnp.zeros_like(hcar)

    xc = _causal_conv(xa_ref[...], xbuf, cw_ref[...], CONV_W, tt, t == 0) + cb_ref[...]

    xcb = xc.astype(BF16)
    r_parts, i_parts = [], []
    for hh in range(tc // HEAD_DIM):
        xh = xcb[:, hh * HEAD_DIM:(hh + 1) * HEAD_DIM]
        r_parts.append(jnp.dot(xh, wr_ref[hh].astype(BF16), preferred_element_type=F32))
        i_parts.append(jnp.dot(xh, wi_ref[hh].astype(BF16), preferred_element_type=F32))
    r = _sigmoid(jnp.concatenate(r_parts, axis=1) + br_ref[...])
    gi = _sigmoid(jnp.concatenate(i_parts, axis=1) + bi_ref[...])

    log_a = (-LRU_C) * r * _softplus(-lam_ref[...])
    a = jnp.exp(log_a)
    u = jnp.sqrt(-jnp.tanh(log_a) * (a * a + 1.0)) * (gi * xc)

    rin = lax.broadcasted_iota(jnp.int32, (tt, tc), 0) & (SUBLANES - 1)
    s = 1
    while s < SUBLANES:
        keep = rin >= s
        a_sh = jnp.where(keep, pltpu.roll(a, s, axis=0), 1.0)
        u_sh = jnp.where(keep, pltpu.roll(u, s, axis=0), 0.0)
        u = a * u_sh + u
        a = a * a_sh
        s *= 2
    carry = hcar[...]
    tiles = []
    for i in range(tt // SUBLANES):
        rows = slice(i * SUBLANES, (i + 1) * SUBLANES)
        h_i = a[rows] * carry + u[rows]
        carry = h_i[SUBLANES - 1:SUBLANES, :]
        tiles.append(h_i)
    h = jnp.concatenate(tiles, axis=0)
    hcar[...] = carry

    ga = ga_ref[...]
    o_ref[...] = (h * (ga * _sigmoid(ga))).astype(o_ref.dtype)


def _rglru(p, bsz, seq, d_lru, cw, cb, w_r, b_r, w_i, b_i, lam, tt=256, tc=512):
    tt = min(tt, seq)
    tc = min(tc, d_lru)
    nt, nc = seq // tt, d_lru // tc
    hpb = tc // HEAD_DIM
    row = lambda v: v.reshape(1, d_lru)
    vec_spec = pl.BlockSpec((1, tc), lambda b, j, t: (0, j))
    return pl.pallas_call(
        functools.partial(_rglru_kernel, tt=tt, tc=tc),
        grid=(bsz, nc, nt),
        in_specs=[
            pl.BlockSpec((tt, tc), lambda b, j, t: (b * nt + t, j)),
            pl.BlockSpec((tt, tc), lambda b, j, t: (b * nt + t, nc + j)),
            pl.BlockSpec((CONV_W, tc), lambda b, j, t: (0, j)),
            vec_spec,
            pl.BlockSpec((hpb, HEAD_DIM, HEAD_DIM), lambda b, j, t: (j, 0, 0)),
            vec_spec,
            pl.BlockSpec((hpb, HEAD_DIM, HEAD_DIM), lambda b, j, t: (j, 0, 0)),
            vec_spec,
            vec_spec,
        ],
        out_specs=pl.BlockSpec((tt, tc), lambda b, j, t: (b * nt + t, j)),
        out_shape=jax.ShapeDtypeStruct((bsz * seq, d_lru), BF16),
        scratch_shapes=[pltpu.VMEM((HIST, tc), F32), pltpu.VMEM((1, tc), F32)],
        compiler_params=_params(("arbitrary", "arbitrary", "arbitrary")),
        name="rglru",
    )(p, p, cw, row(cb), w_r, row(b_r), w_i, row(b_i), row(lam))


def _dn_gates_kernel(tail_ref, alog_ref, dtb_ref, o_ref, *, nheads, tt):
    tail = tail_ref[...]
    lane = lax.broadcasted_iota(jnp.int32, (tt, LANES), 1)
    is_a = (lane >= nheads) & (lane < 2 * nheads)
    g = jnp.where(is_a, -jnp.exp(alog_ref[...]) * _softplus(tail + dtb_ref[...]), 0.0)
    rmod = lax.broadcasted_iota(jnp.int32, (tt, LANES), 0) & (CHUNK - 1)
    s = 1
    while s < CHUNK:
        g = g + jnp.where(rmod >= s, pltpu.roll(g, s, axis=0), 0.0)
        s *= 2
    o_ref[...] = jnp.where(is_a, g, _sigmoid(tail))


def _dn_gates(tail, a_log, dt_bias, nheads, tt=512):
    m = tail.shape[0]
    tt = min(tt, m)
    pad_row = lambda v: jnp.pad(v, (nheads, LANES - 2 * nheads)).reshape(1, LANES)
    return pl.pallas_call(
        functools.partial(_dn_gates_kernel, nheads=nheads, tt=tt),
        grid=(m // tt,),
        in_specs=[pl.BlockSpec((tt, LANES), lambda i: (i, 0)),
                  pl.BlockSpec((1, LANES), lambda i: (0, 0)),
                  pl.BlockSpec((1, LANES), lambda i: (0, 0))],
        out_specs=pl.BlockSpec((tt, LANES), lambda i: (i, 0)),
        out_shape=jax.ShapeDtypeStruct((m, LANES), F32),
        compiler_params=_params(("arbitrary",)),
        name="dn_gates",
    )(tail, pad_row(a_log), pad_row(dt_bias))


def _dn_prep_kernel(q_ref, k_ref, v_ref, gate_ref, cwq_ref, cwk_ref, cwv_ref,
                    u_ref, wq_ref, kdt_ref, attn_ref, gl_ref,
                    qbuf, kbuf, vbuf, *, tt, hg, nheads):
    hp = pl.program_id(1)
    first = pl.program_id(2) == 0

    def conv_silu(ref, buf, cw_ref):
        y = _causal_conv(ref[...], buf, cw_ref[...], CONV_W, tt, first)
        return y * _sigmoid(y)

    q_all = conv_silu(q_ref, qbuf, cwq_ref)
    k_all = conv_silu(k_ref, kbuf, cwk_ref)
    v_all = conv_silu(v_ref, vbuf, cwv_ref)

    gate = gate_ref[...]
    lane = lax.broadcasted_iota(jnp.int32, (tt, LANES), 1)
    ii = lax.broadcasted_iota(jnp.int32, (GROUP, GROUP), 0)
    jj = lax.broadcasted_iota(jnp.int32, (GROUP, GROUP), 1)
    same_chunk = (jnp.bitwise_xor(ii, jj) & CHUNK) == 0
    causal = same_chunk & (ii >= jj)
    strict = same_chunk & (ii > jj)
    eye = jnp.where(ii == jj, 1.0, 0.0)
    top = ii < CHUNK

    chains = []
    for hh in range(hg):
        hd = hp * hg + hh
        cs = slice(hh * HEAD_DIM, (hh + 1) * HEAD_DIM)
        q = q_all[:, cs]
        k = k_all[:, cs]
        v = v_all[:, cs]
        q = (q * lax.rsqrt(jnp.sum(q * q, axis=-1, keepdims=True) + EPS)) * (HEAD_DIM ** -0.5)
        k = k * lax.rsqrt(jnp.sum(k * k, axis=-1, keepdims=True) + EPS)
        beta = jnp.broadcast_to(
            jnp.sum(jnp.where(lane == hd, gate, 0.0), axis=-1, keepdims=True), (tt, LANES))
        gc = jnp.broadcast_to(
            jnp.sum(jnp.where(lane == hd + nheads, gate, 0.0), axis=-1, keepdims=True), (tt, LANES))
        egc = jnp.exp(gc)
        for g in range(tt // GROUP):
            rs = slice(g * GROUP, (g + 1) * GROUP)
            chains.append(dict(cs=cs, g=g, rs=rs, q=q[rs], k=k[rs], v=v[rs], beta=beta[rs],
                               gc=gc[rs], egc=egc[rs]))

    for ch in chains:
        gcg = ch["gc"]
        ch["decay"] = jnp.exp(jnp.where(causal, gcg - gcg.T, -jnp.inf))
        ch["kb"] = ch["k"] * ch["beta"]
    for ch in chains:
        ch["kk"] = _dot(ch["kb"], ch["k"], _NT)
    for ch in chains:
        ch["qk"] = _dot(ch["q"], ch["k"], _NT)
    for ch in chains:
        ch["p"] = -jnp.where(strict, ch["kk"] * ch["decay"], 0.0)
        ch["t"] = eye + ch["p"]
    pw = 2
    while pw < CHUNK:
        for ch in chains:
            ch["p"] = _dot(ch["p"], ch["p"])
        for ch in chains:
            ch["t"] = ch["t"] + _dot(ch["p"], ch["t"])
        pw *= 2
    for ch in chains:
        rhs = jnp.concatenate([ch["v"] * ch["beta"], ch["kb"] * ch["egc"]], axis=1)
        ch["sol"] = _dot(ch["t"], rhs)

    for ch in chains:
        cs, g, rs, gcg = ch["cs"], ch["g"], ch["rs"], ch["gc"]
        lo = g * GROUP
        u = ch["sol"][:, 0:HEAD_DIM]
        w = ch["sol"][:, HEAD_DIM:2 * HEAD_DIM].astype(BF16)
        attn = ch["qk"] * ch["decay"]
        q_dec = (ch["q"] * ch["egc"]).astype(BF16)
        gl0 = gcg[CHUNK - 1:CHUNK, :]
        gl1 = gcg[GROUP - 1:GROUP, :]
        k_dec = ch["k"] * jnp.exp(jnp.where(top, gl0, gl1) - gcg)

        u_ref[rs, cs] = u
        for c in range(2):
            cr = slice(c * CHUNK, (c + 1) * CHUNK)
            wq_ref[2 * lo + c * GROUP:2 * lo + c * GROUP + CHUNK, cs] = w[cr]
            wq_ref[2 * lo + c * GROUP + CHUNK:2 * lo + (c + 1) * GROUP, cs] = q_dec[cr]
        kdt_ref[rs, cs] = k_dec.T.astype(BF16)
        attn_ref[rs, cs] = attn.astype(BF16)
        gl_ref[2 * g:2 * g + 1, cs] = jnp.exp(gl0)
        gl_ref[2 * g + 1:2 * g + 2, cs] = jnp.exp(gl1)


def _dn_scan_kernel(u_ref, wq_ref, kdt_ref, attn_ref, gl_ref, z_ref, nw_ref, o_ref, s_ref,
                    *, tt, hg):
    @pl.when(pl.program_id(2) == 0)
    def _():
        s_ref[...] = jnp.zeros_like(s_ref)

    nw = nw_ref[...]
    zeros = jnp.zeros((CHUNK, HEAD_DIM), F32)
    cols = [slice(hh * HEAD_DIM, (hh + 1) * HEAD_DIM) for hh in range(hg)]
    s_mats = [s_ref[hh] for hh in range(hg)]
    for g in range(tt // GROUP):
        lo = g * GROUP
        rs = slice(lo, lo + GROUP)
        v_prev = [zeros] * hg
        outs = [[] for _ in range(hg)]
        for c in range(2):
            r = [_dot(wq_ref[2 * lo + c * GROUP:2 * lo + (c + 1) * GROUP, cs], s_mats[hh])
                 for hh, cs in enumerate(cols)]
            v_new = [u_ref[lo + c * CHUNK:lo + (c + 1) * CHUNK, cs] - r[hh][0:CHUNK]
                     for hh, cs in enumerate(cols)]
            for hh, cs in enumerate(cols):
                v_upd = jnp.concatenate([zeros, v_new[hh]] if c else [v_new[hh], zeros], axis=0)
                s_mats[hh] = (s_mats[hh] * gl_ref[2 * g + c:2 * g + c + 1, cs]
                              + _dot(kdt_ref[rs, cs], v_upd))
            for hh, cs in enumerate(cols):
                v_att = jnp.concatenate([v_prev[hh], v_new[hh]] if c else [v_new[hh], zeros], axis=0)
                outs[hh].append(r[hh][CHUNK:GROUP]
                                + _dot(attn_ref[lo + c * CHUNK:lo + (c + 1) * CHUNK, cs], v_att))
            v_prev = v_new
        for hh, cs in enumerate(cols):
            o = jnp.concatenate(outs[hh], axis=0)
            ms = jnp.mean(o * o, axis=-1, keepdims=True)
            z = z_ref[rs, cs]
            o_ref[rs, cs] = (((o * lax.rsqrt(ms + EPS)) * nw) * (z * _sigmoid(z))).astype(o_ref.dtype)
    for hh in range(hg):
        s_ref[hh] = s_mats[hh]


def _deltanet(p, gates, bsz, seq, d_lru, d_dn, conv_w3, layer, norm_w, tt=512, hg_prep=2,
              hg_scan=8):
    nheads = d_dn // HEAD_DIM
    m = bsz * seq
    tt = min(tt, seq)
    hg_prep = min(hg_prep, nheads)
    hg_scan = min(hg_scan, nheads)
    nt = seq // tt
    base = 2 * d_lru // HEAD_DIM
    ncb = 2 * tt // GROUP

    def prep_cols(off):
        return pl.BlockSpec((tt, hg_prep * HEAD_DIM),
                            lambda b, h, t: (b * nt + t, (base + off * nheads) // hg_prep + h))

    def prep_cw(off):
        return pl.BlockSpec((None, CONV_W, hg_prep * HEAD_DIM),
                            lambda b, h, t: (layer, 0, off * nheads // hg_prep + h))

    assert base % hg_prep == 0 and nheads % hg_prep == 0 and nheads % hg_scan == 0
    wide = hg_prep * HEAD_DIM
    row_blk = lambda rows: pl.BlockSpec((rows, wide), lambda b, h, t: (b * nt + t, h))
    u, wq, kdt, attn, gl = pl.pallas_call(
        functools.partial(_dn_prep_kernel, tt=tt, hg=hg_prep, nheads=nheads),
        grid=(bsz, nheads // hg_prep, nt),
        in_specs=[prep_cols(0), prep_cols(1), prep_cols(2),
                  pl.BlockSpec((tt, LANES), lambda b, h, t: (b * nt + t, 0)),
                  prep_cw(0), prep_cw(1), prep_cw(2)],
        out_specs=[row_blk(tt), row_blk(2 * tt), row_blk(tt), row_blk(tt), row_blk(ncb)],
        out_shape=[jax.ShapeDtypeStruct((m, d_dn), F32),
                   jax.ShapeDtypeStruct((2 * m, d_dn), BF16),
                   jax.ShapeDtypeStruct((m, d_dn), BF16),
                   jax.ShapeDtypeStruct((m, d_dn), BF16),
                   jax.ShapeDtypeStruct((m // CHUNK, d_dn), F32)],
        scratch_shapes=[pltpu.VMEM((HIST, wide), F32)] * 3,
        compiler_params=_params(("arbitrary", "arbitrary", "arbitrary")),
        name="dn_prep",
    )(p, p, p, gates, conv_w3, conv_w3, conv_w3)

    wide = hg_scan * HEAD_DIM
    row_blk = lambda rows: pl.BlockSpec((rows, wide), lambda b, h, t: (b * nt + t, h))
    z_off = (base + 3 * nheads) // hg_scan
    assert (base + 3 * nheads) % hg_scan == 0
    return pl.pallas_call(
        functools.partial(_dn_scan_kernel, tt=tt, hg=hg_scan),
        grid=(bsz, nheads // hg_scan, nt),
        in_specs=[row_blk(tt), row_blk(2 * tt), row_blk(tt), row_blk(tt), row_blk(ncb),
                  pl.BlockSpec((tt, wide), lambda b, h, t: (b * nt + t, z_off + h)),
                  pl.BlockSpec((1, HEAD_DIM), lambda b, h, t: (0, 0))],
        out_specs=row_blk(tt),
        out_shape=jax.ShapeDtypeStruct((m, d_dn), BF16),
        scratch_shapes=[pltpu.VMEM((hg_scan, HEAD_DIM, HEAD_DIM), F32)],
        compiler_params=_params(("arbitrary", "arbitrary", "arbitrary")),
        name="dn_scan",
    )(u, wq, kdt, attn, gl, p, norm_w.reshape(1, HEAD_DIM))


def _stream_kernel(*refs, n_w, w_is_nk, shortconv, mi, nj, nb, tm, tn, ks, layer,
                   blocks_per_seq):
    a_ref, w_hbm = refs[0], refs[1]
    rest = list(refs[2:])
    cw_ref = rest.pop(0) if shortconv else None
    o_ref, wbf0, wbf1, stage, sem = rest[:5]
    hist = rest[5] if shortconv else None
    wbf = (wbf0, wbf1)
    j = pl.program_id(0)
    i = pl.program_id(1)
    lin = j * mi + i
    n_steps = nj * mi

    def slab_copies(l, slot):
        jb = lax.rem(lax.div(l, mi), nj)
        r = lax.rem(l, mi)
        out = []
        for n in range(n_w):
            if w_is_nk:
                src = w_hbm.at[layer, pl.ds(jb * tn + r * ks, ks), :]
            else:
                src = w_hbm.at[layer, pl.ds(r * ks, ks), pl.ds((n * nb + jb) * tn, tn)]
            out.append(pltpu.make_async_copy(src, stage.at[slot, n], sem.at[slot, n]))
        return out

    def cast_slab(slot, r, dst):
        for n in range(n_w):
            cols = slice(None) if w_is_nk else slice(n * tn, (n + 1) * tn)
            dst[pl.ds(pl.multiple_of(r * ks, ks), ks), cols] = stage[slot, n].astype(BF16)

    @pl.when(lin == 0)
    def _():
        for r in range(mi):
            for c in slab_copies(r, r % 2):
                c.start()
            for c in slab_copies(r, r % 2):
                c.wait()
            cast_slab(r % 2, r, wbf[0])
        for c in slab_copies(mi, 0):
            c.start()

    def body(cur, oth):
        slot = lax.rem(i, 2)
        for c in slab_copies(lin + mi, slot):
            c.wait()
        for c in slab_copies(lin + mi + 1, 1 - slot):
            c.start()
        cast_slab(slot, i, oth)
        if w_is_nk:
            proj = lax.dot_general(a_ref[...], cur[...], _NT, preferred_element_type=F32)
        else:
            proj = jnp.dot(a_ref[...], cur[...], preferred_element_type=F32)
        if shortconv:
            xin, gate_b, gate_c, z = (proj[:, n * tn:(n + 1) * tn] for n in range(4))
            conv = _causal_conv(gate_c * xin, hist, cw_ref[...], SC_CONV_W, tm,
                                lax.rem(i, blocks_per_seq) == 0)
            proj = (gate_b * conv) * (z * _sigmoid(z))
        o_ref[...] = proj.astype(o_ref.dtype)

    @pl.when(lax.rem(j, 2) == 0)
    def _():
        body(wbf[0], wbf[1])

    @pl.when(lax.rem(j, 2) == 1)
    def _():
        body(wbf[1], wbf[0])

    @pl.when(lin == n_steps - 1)
    def _():
        for c in slab_copies(lin + mi + 1, 1 - lax.rem(i, 2)):
            c.wait()


def _stream_matmul(a, w3, layer, n_cols, *, w_is_nk=False, conv_w3=None, seq=None,
                   out_dtype=F32, tm=1024, tn=1024, name="stream_matmul"):
    m, k = a.shape
    shortconv = conv_w3 is not None
    n_w = 4 if shortconv else 1
    tm = min(tm, seq if shortconv else m)
    if (m // tm) % 2:
        tm //= 2
    tn = max(t for t in range(LANES, min(tn, n_cols) + 1, LANES) if n_cols % t == 0)
    mi, nj = m // tm, n_cols // tn
    assert m % tm == 0 and mi % 2 == 0 and not (shortconv and w_is_nk)
    if w_is_nk:
        ks = tn // mi
        wbf_shape, stage_shape = (tn, k), (2, 1, ks, k)
        assert tn % mi == 0 and ks % 16 == 0
    else:
        ks = k // mi
        wbf_shape, stage_shape = (k, n_w * tn), (2, n_w, ks, tn)
        assert k % mi == 0 and ks % 16 == 0
    in_specs = [pl.BlockSpec((tm, k), lambda j, i: (i, 0)),
                pl.BlockSpec(memory_space=pl.ANY)]
    args = [a, w3]
    scratch = [pltpu.VMEM(wbf_shape, BF16), pltpu.VMEM(wbf_shape, BF16),
               pltpu.VMEM(stage_shape, F32), pltpu.SemaphoreType.DMA((2, n_w))]
    if shortconv:
        in_specs.append(pl.BlockSpec((None, SC_CONV_W, tn), lambda j, i: (layer, 0, j)))
        args.append(conv_w3)
        scratch.append(pltpu.VMEM((HIST, tn), F32))
    return pl.pallas_call(
        functools.partial(_stream_kernel, n_w=n_w, w_is_nk=w_is_nk, shortconv=shortconv, mi=mi,
                          nj=nj, nb=nj, tm=tm, tn=tn, ks=ks, layer=layer,
                          blocks_per_seq=(seq // tm) if shortconv else 1),
        grid=(nj, mi),
        in_specs=in_specs,
        out_specs=pl.BlockSpec((tm, tn), lambda j, i: (i, j)),
        out_shape=jax.ShapeDtypeStruct((m, n_cols), out_dtype),
        scratch_shapes=scratch,
        compiler_params=_params(("arbitrary", "arbitrary")),
        name=name,
    )(*args)


def kernel(x, even_norm_w, even_w_in, lru_conv_w, lru_conv_b, lru_w_r, lru_b_r, lru_w_i, lru_b_i,
           lru_lambda, dn_conv_w, dn_a_log, dn_dt_bias, dn_norm_w, even_w_out, odd_norm_w,
           odd_w_in, odd_conv_w, odd_w_out, final_norm_w):
    bsz, seq, d = x.shape
    x2 = x.reshape(bsz * seq, d)
    d_lru = lru_lambda.shape[1]
    d_dn = dn_conv_w.shape[2] // 3
    nheads = d_dn // HEAD_DIM
    n_main = 2 * d_lru + 4 * d_dn
    depth = even_norm_w.shape[0] + odd_norm_w.shape[0]
    for layer in range(depth):
        j = layer // 2
        if layer % 2 == 0:
            h = _rmsnorm(x2, even_norm_w[j], BF16)
            w_in_nk = jnp.swapaxes(even_w_in, 1, 2)
            p = _stream_matmul(h, w_in_nk, j, n_main, w_is_nk=True, name="in_proj0")
            w_tail = jnp.pad(w_in_nk[j, n_main:, :], ((0, LANES - 2 * nheads), (0, 0)))[None]
            tail = _matmul([h], w_tail, 0, LANES, w_is_nk=True, name="in_proj0_tail")
            gates = _dn_gates(tail, dn_a_log[j], dn_dt_bias[j], nheads)
            ya = _rglru(p, bsz, seq, d_lru, lru_conv_w[j], lru_conv_b[j], lru_w_r[j], lru_b_r[j],
                        lru_w_i[j], lru_b_i[j], lru_lambda[j])
            yb = _deltanet(p, gates, bsz, seq, d_lru, d_dn, dn_conv_w, j, dn_norm_w[j])
            x2 = _matmul([ya, yb], even_w_out, j, d, res=x2, tn=RES_TN, name="out_proj0")
        else:
            h = _rmsnorm(x2, odd_norm_w[j], BF16)
            y = _stream_matmul(h, odd_w_in, j, odd_conv_w.shape[2], conv_w3=odd_conv_w, seq=seq,
                               out_dtype=BF16, tn=SC_TN, name="shortconv")
            x2 = _matmul([y], odd_w_out, j, d, res=x2, tn=RES_TN, name="out_proj1")
    return _rmsnorm(x2, final_norm_w, x.dtype).reshape(bsz, seq, d)
```

```python
import functools

import jax
import jax.numpy as jnp
from jax import lax
from jax.experimental import pallas as pl
from jax.experimental.pallas import tpu as pltpu

HEAD_DIM = 128
CHUNK = 64
GROUP = 2 * CHUNK
LANES = 128
SUBLANES = 8
CONV_W = 4
SC_CONV_W = 3
LRU_C = 8.0
EPS = 1e-6
HIST = 8
VMEM_LIMIT = 56 * 1024 * 1024
VMEM_LIMIT_RES = 61 * 1024 * 1024
SC_TN = 256
VMEM_SLACK = 4 * 1024 * 1024
RES_TN = 512

F32 = jnp.float32
BF16 = jnp.bfloat16

assert GROUP == HEAD_DIM == LANES


def _params(sem, vmem=VMEM_LIMIT):
    return pltpu.CompilerParams(dimension_semantics=sem, vmem_limit_bytes=vmem)


def _sigmoid(x):
    return jax.nn.sigmoid(x)


def _softplus(x):
    return jnp.maximum(x, 0.0) + jnp.log1p(jnp.exp(-jnp.abs(x)))


def _dot(a, b, dims=(((1,), (0,)), ((), ()))):
    return lax.dot_general(a.astype(BF16), b.astype(BF16), dims, preferred_element_type=F32)


_NT = (((1,), (1,)), ((), ()))


def _rmsnorm_kernel(x_ref, w_ref, o_ref):
    x = x_ref[...]
    ms = jnp.mean(x * x, axis=-1, keepdims=True)
    o_ref[...] = ((x * lax.rsqrt(ms + EPS)) * w_ref[...]).astype(o_ref.dtype)


def _rmsnorm(x2, w, out_dtype, tm=256):
    m, d = x2.shape
    tm = min(tm, m)
    return pl.pallas_call(
        _rmsnorm_kernel,
        grid=(m // tm,),
        in_specs=[pl.BlockSpec((tm, d), lambda i: (i, 0)),
                  pl.BlockSpec((1, d), lambda i: (0, 0))],
        out_specs=pl.BlockSpec((tm, d), lambda i: (i, 0)),
        out_shape=jax.ShapeDtypeStruct((m, d), out_dtype),
        compiler_params=_params(("arbitrary",)),
        name="rmsnorm",
    )(x2, w.reshape(1, d))


def _mm_kernel(*refs, n_a, has_res, w_is_nk):
    a_refs = refs[:n_a]
    w_ref = refs[n_a]
    r_ref = refs[n_a + 1] if has_res else None
    o_ref, wbf_ref = refs[-2], refs[-1]

    @pl.when(pl.program_id(1) == 0)
    def _():
        wbf_ref[...] = w_ref[...].astype(BF16)

    acc = None
    off = 0
    for a_ref in a_refs:
        kk = a_ref.shape[1]
        if w_is_nk:
            part = lax.dot_general(a_ref[...], wbf_ref[:, off:off + kk], _NT,
                                   preferred_element_type=F32)
        else:
            part = jnp.dot(a_ref[...], wbf_ref[off:off + kk, :], preferred_element_type=F32)
        acc = part if acc is None else acc + part
        off += kk
    if has_res:
        acc = acc + r_ref[...]
    o_ref[...] = acc.astype(o_ref.dtype)


def _matmul(a_parts, w3, layer, n_cols, res=None, out_dtype=F32, tm=1024, tn=1024,
            w_is_nk=False, name="matmul"):
    m = a_parts[0].shape[0]
    k = w3.shape[2] if w_is_nk else w3.shape[1]
    assert sum(a.shape[1] for a in a_parts) == k
    tm = min(tm, m)
    tn = max(t for t in range(LANES, min(tn, n_cols) + 1, LANES) if n_cols % t == 0)
    assert m % tm == 0
    in_specs = [pl.BlockSpec((tm, a.shape[1]), lambda j, i: (i, 0)) for a in a_parts]
    out_bytes = tm * tn * jnp.dtype(out_dtype).itemsize
    tiles = 2 * (tm * k * 2 + out_bytes + (tm * tn * 4 if res is not None else 0)) + k * tn * 2
    w_mode = pl.Buffered(1) if tiles + 2 * k * tn * 4 > VMEM_LIMIT - VMEM_SLACK else None
    if w_is_nk:
        in_specs.append(pl.BlockSpec((None, tn, k), lambda j, i: (layer, j, 0),
                                     pipeline_mode=w_mode))
    else:
        in_specs.append(pl.BlockSpec((None, k, tn), lambda j, i: (layer, 0, j),
                                     pipeline_mode=w_mode))
    args = list(a_parts) + [w3]
    if res is not None:
        in_specs.append(pl.BlockSpec((tm, tn), lambda j, i: (i, j)))
        args.append(res)
    return pl.pallas_call(
        functools.partial(_mm_kernel, n_a=len(a_parts), has_res=res is not None, w_is_nk=w_is_nk),
        grid=(n_cols // tn, m // tm),
        in_specs=in_specs,
        out_specs=pl.BlockSpec((tm, tn), lambda j, i: (i, j)),
        out_shape=jax.ShapeDtypeStruct((m, n_cols), out_dtype),
        scratch_shapes=[pltpu.VMEM((tn, k) if w_is_nk else (k, tn), BF16)],
        compiler_params=_params(("arbitrary", "arbitrary")),
        name=name,
    )(*args)


def _causal_conv(x_raw, hist, cw, width, tt, first):
    @pl.when(first)
    def _():
        hist[...] = jnp.zeros_like(hist)

    ext = jnp.concatenate([hist[...], x_raw], axis=0)
    hist[...] = x_raw[tt - HIST:tt, :]
    z = cw[0:1, :] * ext
    for kk in range(1, width):
        z = cw[kk:kk + 1, :] * ext + pltpu.roll(z, 1, axis=0)
    return z[HIST:HIST + tt, :]


def _rglru_kernel(xa_ref, ga_ref, cw_ref, cb_ref, wr_ref, br_ref, wi_ref, bi_ref, lam_ref,
                  o_ref, xbuf, hcar, *, tt, tc):
    t = pl.program_id(2)

    @pl.when(t == 0)
    def _():
        hcar[...] = jnp.zeros_like(hcar)

    xc = _causal_conv(xa_ref[...], xbuf, cw_ref[...], CONV_W, tt, t == 0) + cb_ref[...]

    xcb = xc.astype(BF16)
    r_parts, i_parts = [], []
    for hh in range(tc // HEAD_DIM):
        xh = xcb[:, hh * HEAD_DIM:(hh + 1) * HEAD_DIM]
        r_parts.append(jnp.dot(xh, wr_ref[hh].astype(BF16), preferred_element_type=F32))
        i_parts.append(jnp.dot(xh, wi_ref[hh].astype(BF16), preferred_element_type=F32))
    r = _sigmoid(jnp.concatenate(r_parts, axis=1) + br_ref[...])
    gi = _sigmoid(jnp.concatenate(i_parts, axis=1) + bi_ref[...])

    log_a = (-LRU_C) * r * _softplus(-lam_ref[...])
    a = jnp.exp(log_a)
    u = jnp.sqrt(-jnp.tanh(log_a) * (a * a + 1.0)) * (gi * xc)

    rin = lax.broadcasted_iota(jnp.int32, (tt, tc), 0) & (SUBLANES - 1)
    s = 1
    while s < SUBLANES:
        keep = rin >= s
        a_sh = jnp.where(keep, pltpu.roll(a, s, axis=0), 1.0)
        u_sh = jnp.where(keep, pltpu.roll(u, s, axis=0), 0.0)
        u = a * u_sh + u
        a = a * a_sh
        s *= 2
    carry = hcar[...]
    tiles = []
    for i in range(tt // SUBLANES):
        rows = slice(i * SUBLANES, (i + 1) * SUBLANES)
        h_i = a[rows] * carry + u[rows]
        carry = h_i[SUBLANES - 1:SUBLANES, :]
        tiles.append(h_i)
    h = jnp.concatenate(tiles, axis=0)
    hcar[...] = carry

    ga = ga_ref[...]
    o_ref[...] = (h * (ga * _sigmoid(ga))).astype(o_ref.dtype)


def _rglru(p, bsz, seq, d_lru, cw, cb, w_r, b_r, w_i, b_i, lam, tt=256, tc=512):
    tt = min(tt, seq)
    tc = min(tc, d_lru)
    nt, nc = seq // tt, d_lru // tc
    hpb = tc // HEAD_DIM
    row = lambda v: v.reshape(1, d_lru)
    vec_spec = pl.BlockSpec((1, tc), lambda b, j, t: (0, j))
    return pl.pallas_call(
        functools.partial(_rglru_kernel, tt=tt, tc=tc),
        grid=(bsz, nc, nt),
        in_specs=[
            pl.BlockSpec((tt, tc), lambda b, j, t: (b * nt + t, j)),
            pl.BlockSpec((tt, tc), lambda b, j, t: (b * nt + t, nc + j)),
            pl.BlockSpec((CONV_W, tc), lambda b, j, t: (0, j)),
            vec_spec,
            pl.BlockSpec((hpb, HEAD_DIM, HEAD_DIM), lambda b, j, t: (j, 0, 0)),
            vec_spec,
            pl.BlockSpec((hpb, HEAD_DIM, HEAD_DIM), lambda b, j, t: (j, 0, 0)),
            vec_spec,
            vec_spec,
        ],
        out_specs=pl.BlockSpec((tt, tc), lambda b, j, t: (b * nt + t, j)),
        out_shape=jax.ShapeDtypeStruct((bsz * seq, d_lru), BF16),
        scratch_shapes=[pltpu.VMEM((HIST, tc), F32), pltpu.VMEM((1, tc), F32)],
        compiler_params=_params(("arbitrary", "arbitrary", "arbitrary")),
        name="rglru",
    )(p, p, cw, row(cb), w_r, row(b_r), w_i, row(b_i), row(lam))


def _dn_gates_kernel(tail_ref, alog_ref, dtb_ref, o_ref, *, nheads, tt):
    tail = tail_ref[...]
    lane = lax.broadcasted_iota(jnp.int32, (tt, LANES), 1)
    is_a = (lane >= nheads) & (lane < 2 * nheads)
    g = jnp.where(is_a, -jnp.exp(alog_ref[...]) * _softplus(tail + dtb_ref[...]), 0.0)
    rmod = lax.broadcasted_iota(jnp.int32, (tt, LANES), 0) & (CHUNK - 1)
    s = 1
    while s < CHUNK:
        g = g + jnp.where(rmod >= s, pltpu.roll(g, s, axis=0), 0.0)
        s *= 2
    o_ref[...] = jnp.where(is_a, g, _sigmoid(tail))


def _dn_gates(tail, a_log, dt_bias, nheads, tt=512):
    m = tail.shape[0]
    tt = min(tt, m)
    pad_row = lambda v: jnp.pad(v, (nheads, LANES - 2 * nheads)).reshape(1, LANES)
    return pl.pallas_call(
        functools.partial(_dn_gates_kernel, nheads=nheads, tt=tt),
        grid=(m // tt,),
        in_specs=[pl.BlockSpec((tt, LANES), lambda i: (i, 0)),
                  pl.BlockSpec((1, LANES), lambda i: (0, 0)),
                  pl.BlockSpec((1, LANES), lambda i: (0, 0))],
        out_specs=pl.BlockSpec((tt, LANES), lambda i: (i, 0)),
        out_shape=jax.ShapeDtypeStruct((m, LANES), F32),
        compiler_params=_params(("arbitrary",)),
        name="dn_gates",
    )(tail, pad_row(a_log), pad_row(dt_bias))


def _dn_prep_kernel(q_ref, k_ref, v_ref, gate_ref, cwq_ref, cwk_ref, cwv_ref,
                    u_ref, wq_ref, kdt_ref, attn_ref, gl_ref,
                    qbuf, kbuf, vbuf, *, tt, hg, nheads):
    hp = pl.program_id(1)
    first = pl.program_id(2) == 0

    @pl.when(first)
    def _():
        for buf in (qbuf, kbuf, vbuf):
            buf[...] = jnp.zeros_like(buf)

    def conv_silu(ref, buf, cw_ref, lo, cs):
        if lo == 0:
            ext = jnp.concatenate([buf[:, cs], ref[0:GROUP, cs]], axis=0)
        else:
            ext = ref[lo - HIST:lo + GROUP, cs]
        z = cw_ref[0:1, cs] * ext
        for kk in range(1, CONV_W):
            z = cw_ref[kk:kk + 1, cs] * ext + pltpu.roll(z, 1, axis=0)
        y = z[HIST:HIST + GROUP, :]
        return y * _sigmoid(y)

    lane = lax.broadcasted_iota(jnp.int32, (GROUP, LANES), 1)
    ii = lax.broadcasted_iota(jnp.int32, (GROUP, GROUP), 0)
    jj = lax.broadcasted_iota(jnp.int32, (GROUP, GROUP), 1)
    same_chunk = (jnp.bitwise_xor(ii, jj) & CHUNK) == 0
    causal = same_chunk & (ii >= jj)
    strict = same_chunk & (ii > jj)
    eye = jnp.where(ii == jj, 1.0, 0.0)
    top = ii < CHUNK

    chains = []
    for hh in range(hg):
        hd = hp * hg + hh
        cs = slice(hh * HEAD_DIM, (hh + 1) * HEAD_DIM)
        for g in range(tt // GROUP):
            lo = g * GROUP
            rs = slice(lo, lo + GROUP)
            q = conv_silu(q_ref, qbuf, cwq_ref, lo, cs)
            k = conv_silu(k_ref, kbuf, cwk_ref, lo, cs)
            v = conv_silu(v_ref, vbuf, cwv_ref, lo, cs)
            q = (q * lax.rsqrt(jnp.sum(q * q, axis=-1, keepdims=True) + EPS)) * (HEAD_DIM ** -0.5)
            k = k * lax.rsqrt(jnp.sum(k * k, axis=-1, keepdims=True) + EPS)
            gate = gate_ref[rs, :]
            beta = jnp.broadcast_to(
                jnp.sum(jnp.where(lane == hd, gate, 0.0), axis=-1, keepdims=True), (GROUP, LANES))
            gc = jnp.broadcast_to(
                jnp.sum(jnp.where(lane == hd + nheads, gate, 0.0), axis=-1, keepdims=True),
                (GROUP, LANES))
            chains.append(dict(cs=cs, g=g, rs=rs, q=q, k=k, v=v, beta=beta, gc=gc,
                               egc=jnp.exp(gc)))
    for ref, buf in ((q_ref, qbuf), (k_ref, kbuf), (v_ref, vbuf)):
        buf[...] = ref[tt - HIST:tt, :]

    for ch in chains:
        gcg = ch["gc"]
        ch["decay"] = jnp.exp(jnp.where(causal, gcg - gcg.T, -jnp.inf))
        ch["kb"] = ch["k"] * ch["beta"]
    for ch in chains:
        ch["kk"] = _dot(ch["kb"], ch["k"], _NT)
    for ch in chains:
        ch["qk"] = _dot(ch["q"], ch["k"], _NT)
    for ch in chains:
        ch["p"] = -jnp.where(strict, ch["kk"] * ch["decay"], 0.0)
        ch["t"] = eye + ch["p"]
    pw = 2
    while pw < CHUNK:
        for ch in chains:
            ch["p"] = _dot(ch["p"], ch["p"])
        for ch in chains:
            ch["t"] = ch["t"] + _dot(ch["p"], ch["t"])
        pw *= 2
    for ch in chains:
        rhs = jnp.concatenate([ch["v"] * ch["beta"], ch["kb"] * ch["egc"]], axis=1)
        ch["sol"] = _dot(ch["t"], rhs)

    for ch in chains:
        cs, g, rs, gcg = ch["cs"], ch["g"], ch["rs"], ch["gc"]
        lo = g * GROUP
        u = ch["sol"][:, 0:HEAD_DIM]
        w = ch["sol"][:, HEAD_DIM:2 * HEAD_DIM].astype(BF16)
        attn = ch["qk"] * ch["decay"]
        q_dec = (ch["q"] * ch["egc"]).astype(BF16)
        gl0 = gcg[CHUNK - 1:CHUNK, :]
        gl1 = gcg[GROUP - 1:GROUP, :]
        k_dec = ch["k"] * jnp.exp(jnp.where(top, gl0, gl1) - gcg)

        u_ref[rs, cs] = u
        for c in range(2):
            cr = slice(c * CHUNK, (c + 1) * CHUNK)
            wq_ref[2 * lo + c * GROUP:2 * lo + c * GROUP + CHUNK, cs] = w[cr]
            wq_ref[2 * lo + c * GROUP + CHUNK:2 * lo + (c + 1) * GROUP, cs] = q_dec[cr]
        kdt_ref[rs, cs] = k_dec.T.astype(BF16)
        attn_ref[rs, cs] = attn.astype(BF16)
        gl_ref[2 * g:2 * g + 1, cs] = jnp.exp(gl0)
        gl_ref[2 * g + 1:2 * g + 2, cs] = jnp.exp(gl1)


def _dn_scan_kernel(u_ref, wq_ref, kdt_ref, attn_ref, gl_ref, z_ref, nw_ref, o_ref, s_ref,
                    *, tt, hg):
    @pl.when(pl.program_id(2) == 0)
    def _():
        s_ref[...] = jnp.zeros_like(s_ref)

    nw = nw_ref[...]
    zeros = jnp.zeros((CHUNK, HEAD_DIM), F32)
    cols = [slice(hh * HEAD_DIM, (hh + 1) * HEAD_DIM) for hh in range(hg)]
    s_mats = [s_ref[hh] for hh in range(hg)]
    for g in range(tt // GROUP):
        lo = g * GROUP
        rs = slice(lo, lo + GROUP)
        v_prev = [zeros] * hg
        outs = [[] for _ in range(hg)]
        for c in range(2):
            r = [_dot(wq_ref[2 * lo + c * GROUP:2 * lo + (c + 1) * GROUP, cs], s_mats[hh])
                 for hh, cs in enumerate(cols)]
            v_new = [u_ref[lo + c * CHUNK:lo + (c + 1) * CHUNK, cs] - r[hh][0:CHUNK]
                     for hh, cs in enumerate(cols)]
            for hh, cs in enumerate(cols):
                v_upd = jnp.concatenate([zeros, v_new[hh]] if c else [v_new[hh], zeros], axis=0)
                s_mats[hh] = (s_mats[hh] * gl_ref[2 * g + c:2 * g + c + 1, cs]
                              + _dot(kdt_ref[rs, cs], v_upd))
            for hh, cs in enumerate(cols):
                v_att = jnp.concatenate([v_prev[hh], v_new[hh]] if c else [v_new[hh], zeros], axis=0)
                outs[hh].append(r[hh][CHUNK:GROUP]
                                + _dot(attn_ref[lo + c * CHUNK:lo + (c + 1) * CHUNK, cs], v_att))
            v_prev = v_new
        for hh, cs in enumerate(cols):
            o = jnp.concatenate(outs[hh], axis=0)
            ms = jnp.mean(o * o, axis=-1, keepdims=True)
            z = z_ref[rs, cs]
            o_ref[rs, cs] = (((o * lax.rsqrt(ms + EPS)) * nw) * (z * _sigmoid(z))).astype(o_ref.dtype)
    for hh in range(hg):
        s_ref[hh] = s_mats[hh]


def _deltanet(p, gates, bsz, seq, d_lru, d_dn, conv_w3, layer, norm_w, tt=512, hg_prep=2,
              hg_scan=8):
    nheads = d_dn // HEAD_DIM
    m = bsz * seq
    tt = min(tt, seq)
    hg_prep = min(hg_prep, nheads)
    hg_scan = min(hg_scan, nheads)
    nt = seq // tt
    base = 2 * d_lru // HEAD_DIM
    ncb = 2 * tt // GROUP

    def prep_cols(off):
        return pl.BlockSpec((tt, hg_prep * HEAD_DIM),
                            lambda b, h, t: (b * nt + t, (base + off * nheads) // hg_prep + h))

    def prep_cw(off):
        return pl.BlockSpec((None, CONV_W, hg_prep * HEAD_DIM),
                            lambda b, h, t: (layer, 0, off * nheads // hg_prep + h))

    assert base % hg_prep == 0 and nheads % hg_prep == 0 and nheads % hg_scan == 0
    wide = hg_prep * HEAD_DIM
    row_blk = lambda rows: pl.BlockSpec((rows, wide), lambda b, h, t: (b * nt + t, h))
    u, wq, kdt, attn, gl = pl.pallas_call(
        functools.partial(_dn_prep_kernel, tt=tt, hg=hg_prep, nheads=nheads),
        grid=(bsz, nheads // hg_prep, nt),
        in_specs=[prep_cols(0), prep_cols(1), prep_cols(2),
                  pl.BlockSpec((tt, LANES), lambda b, h, t: (b * nt + t, 0)),
                  prep_cw(0), prep_cw(1), prep_cw(2)],
        out_specs=[row_blk(tt), row_blk(2 * tt), row_blk(tt), row_blk(tt), row_blk(ncb)],
        out_shape=[jax.ShapeDtypeStruct((m, d_dn), F32),
                   jax.ShapeDtypeStruct((2 * m, d_dn), BF16),
                   jax.ShapeDtypeStruct((m, d_dn), BF16),
                   jax.ShapeDtypeStruct((m, d_dn), BF16),
                   jax.ShapeDtypeStruct((m // CHUNK, d_dn), F32)],
        scratch_shapes=[pltpu.VMEM((HIST, wide), F32)] * 3,
        compiler_params=_params(("arbitrary", "arbitrary", "arbitrary")),
        name="dn_prep",
    )(p, p, p, gates, conv_w3, conv_w3, conv_w3)

    wide = hg_scan * HEAD_DIM
    row_blk = lambda rows: pl.BlockSpec((rows, wide), lambda b, h, t: (b * nt + t, h))
    z_off = (base + 3 * nheads) // hg_scan
    assert (base + 3 * nheads) % hg_scan == 0
    return pl.pallas_call(
        functools.partial(_dn_scan_kernel, tt=tt, hg=hg_scan),
        grid=(bsz, nheads // hg_scan, nt),
        in_specs=[row_blk(tt), row_blk(2 * tt), row_blk(tt), row_blk(tt), row_blk(ncb),
                  pl.BlockSpec((tt, wide), lambda b, h, t: (b * nt + t, z_off + h)),
                  pl.BlockSpec((1, HEAD_DIM), lambda b, h, t: (0, 0))],
        out_specs=row_blk(tt),
        out_shape=jax.ShapeDtypeStruct((m, d_dn), BF16),
        scratch_shapes=[pltpu.VMEM((hg_scan, HEAD_DIM, HEAD_DIM), F32)],
        compiler_params=_params(("arbitrary", "arbitrary", "arbitrary")),
        name="dn_scan",
    )(u, wq, kdt, attn, gl, p, norm_w.reshape(1, HEAD_DIM))


def _stream_kernel(*refs, n_a, n_w, w_is_nk, shortconv, has_res, mi, nj, nb, tm, tn, ks, layer,
                   blocks_per_seq):
    a_refs, w_hbm = refs[:n_a], refs[n_a]
    rest = list(refs[n_a + 1:])
    cw_ref = rest.pop(0) if shortconv else None
    r_ref = rest.pop(0) if has_res else None
    o_ref, wbf0, wbf1, stage, sem = rest[:5]
    hist = rest[5] if shortconv else None
    wbf = (wbf0, wbf1)
    j = pl.program_id(0)
    i = pl.program_id(1)
    lin = j * mi + i
    n_steps = nj * mi

    def slab_copies(l, slot):
        jb = lax.rem(lax.div(l, mi), nj)
        r = lax.rem(l, mi)
        out = []
        for n in range(n_w):
            if w_is_nk:
                src = w_hbm.at[layer, pl.ds(jb * tn + r * ks, ks), :]
            else:
                src = w_hbm.at[layer, pl.ds(r * ks, ks), pl.ds((n * nb + jb) * tn, tn)]
            out.append(pltpu.make_async_copy(src, stage.at[slot, n], sem.at[slot, n]))
        return out

    def cast_slab(slot, r, dst):
        for n in range(n_w):
            cols = slice(None) if w_is_nk else slice(n * tn, (n + 1) * tn)
            dst[pl.ds(pl.multiple_of(r * ks, ks), ks), cols] = stage[slot, n].astype(BF16)

    @pl.when(lin == 0)
    def _():
        for r in range(mi):
            for c in slab_copies(r, r % 2):
                c.start()
            for c in slab_copies(r, r % 2):
                c.wait()
            cast_slab(r % 2, r, wbf[0])
        for c in slab_copies(mi, 0):
            c.start()

    def body(cur, oth):
        slot = lax.rem(i, 2)
        for c in slab_copies(lin + mi, slot):
            c.wait()
        for c in slab_copies(lin + mi + 1, 1 - slot):
            c.start()
        cast_slab(slot, i, oth)
        proj = None
        off = 0
        for a_ref in a_refs:
            kk = a_ref.shape[1]
            if w_is_nk:
                part = lax.dot_general(a_ref[...], cur[:, off:off + kk], _NT,
                                       preferred_element_type=F32)
            else:
                part = jnp.dot(a_ref[...], cur[off:off + kk, :], preferred_element_type=F32)
            proj = part if proj is None else proj + part
            off += kk
        if has_res:
            proj = proj + r_ref[...]
        if shortconv:
            xin, gate_b, gate_c, z = (proj[:, n * tn:(n + 1) * tn] for n in range(4))
            conv = _causal_conv(gate_c * xin, hist, cw_ref[...], SC_CONV_W, tm,
                                lax.rem(i, blocks_per_seq) == 0)
            proj = (gate_b * conv) * (z * _sigmoid(z))
        o_ref[...] = proj.astype(o_ref.dtype)

    @pl.when(lax.rem(j, 2) == 0)
    def _():
        body(wbf[0], wbf[1])

    @pl.when(lax.rem(j, 2) == 1)
    def _():
        body(wbf[1], wbf[0])

    @pl.when(lin == n_steps - 1)
    def _():
        for c in slab_copies(lin + mi + 1, 1 - lax.rem(i, 2)):
            c.wait()


def _stream_matmul(a_parts, w3, layer, n_cols, *, w_is_nk=False, conv_w3=None, seq=None, res=None,
                   out_dtype=F32, tm=1024, tn=1024, name="stream_matmul"):
    m = a_parts[0].shape[0]
    k = sum(a.shape[1] for a in a_parts)
    shortconv = conv_w3 is not None
    n_w = 4 if shortconv else 1
    tm = min(tm, seq if shortconv else m)
    if (m // tm) % 2:
        tm //= 2
    tn = max(t for t in range(LANES, min(tn, n_cols) + 1, LANES) if n_cols % t == 0)
    mi, nj = m // tm, n_cols // tn
    assert m % tm == 0 and mi % 2 == 0 and not (shortconv and w_is_nk)
    if w_is_nk:
        ks = tn // mi
        wbf_shape, stage_shape = (tn, k), (2, 1, ks, k)
        assert tn % mi == 0 and ks % 16 == 0
    else:
        ks = k // mi
        wbf_shape, stage_shape = (k, n_w * tn), (2, n_w, ks, tn)
        assert k % mi == 0 and ks % 16 == 0
    in_specs = [pl.BlockSpec((tm, a.shape[1]), lambda j, i: (i, 0)) for a in a_parts]
    in_specs.append(pl.BlockSpec(memory_space=pl.ANY))
    args = list(a_parts) + [w3]
    scratch = [pltpu.VMEM(wbf_shape, BF16), pltpu.VMEM(wbf_shape, BF16),
               pltpu.VMEM(stage_shape, F32), pltpu.SemaphoreType.DMA((2, n_w))]
    if shortconv:
        in_specs.append(pl.BlockSpec((None, SC_CONV_W, tn), lambda j, i: (layer, 0, j)))
        args.append(conv_w3)
        scratch.append(pltpu.VMEM((HIST, tn), F32))
    if res is not None:
        in_specs.append(pl.BlockSpec((tm, tn), lambda j, i: (i, j)))
        args.append(res)
    return pl.pallas_call(
        functools.partial(_stream_kernel, n_a=len(a_parts), n_w=n_w, w_is_nk=w_is_nk,
                          shortconv=shortconv, has_res=res is not None, mi=mi,
                          nj=nj, nb=nj, tm=tm, tn=tn, ks=ks, layer=layer,
                          blocks_per_seq=(seq // tm) if shortconv else 1),
        grid=(nj, mi),
        in_specs=in_specs,
        out_specs=pl.BlockSpec((tm, tn), lambda j, i: (i, j)),
        out_shape=jax.ShapeDtypeStruct((m, n_cols), out_dtype),
        scratch_shapes=scratch,
        compiler_params=_params(("arbitrary", "arbitrary"),
                                VMEM_LIMIT_RES if res is not None else VMEM_LIMIT),
        name=name,
    )(*args)


def kernel(x, even_norm_w, even_w_in, lru_conv_w, lru_conv_b, lru_w_r, lru_b_r, lru_w_i, lru_b_i,
           lru_lambda, dn_conv_w, dn_a_log, dn_dt_bias, dn_norm_w, even_w_out, odd_norm_w,
           odd_w_in, odd_conv_w, odd_w_out, final_norm_w):
    bsz, seq, d = x.shape
    x2 = x.reshape(bsz * seq, d)
    d_lru = lru_lambda.shape[1]
    d_dn = dn_conv_w.shape[2] // 3
    nheads = d_dn // HEAD_DIM
    n_main = 2 * d_lru + 4 * d_dn
    depth = even_norm_w.shape[0] + odd_norm_w.shape[0]
    for layer in range(depth):
        j = layer // 2
        if layer % 2 == 0:
            h = _rmsnorm(x2, even_norm_w[j], BF16)
            w_in_nk = jnp.swapaxes(even_w_in, 1, 2)
            p = _stream_matmul([h], w_in_nk, j, n_main, w_is_nk=True, name="in_proj0")
            w_tail = jnp.pad(w_in_nk[j, n_main:, :], ((0, LANES - 2 * nheads), (0, 0)))[None]
            tail = _matmul([h], w_tail, 0, LANES, w_is_nk=True, name="in_proj0_tail")
            gates = _dn_gates(tail, dn_a_log[j], dn_dt_bias[j], nheads)
            ya = _rglru(p, bsz, seq, d_lru, lru_conv_w[j], lru_conv_b[j], lru_w_r[j], lru_b_r[j],
                        lru_w_i[j], lru_b_i[j], lru_lambda[j])
            yb = _deltanet(p, gates, bsz, seq, d_lru, d_dn, dn_conv_w, j, dn_norm_w[j])
            x2 = _stream_matmul([ya, yb], even_w_out, j, d, res=x2, name="out_proj0")
        else:
            h = _rmsnorm(x2, odd_norm_w[j], BF16)
            y = _stream_matmul([h], odd_w_in, j, odd_conv_w.shape[2], conv_w3=odd_conv_w, seq=seq,
                               out_dtype=BF16, tn=SC_TN, name="shortconv")
            x2 = _stream_matmul([y], odd_w_out, j, d, res=x2, name="out_proj1")
    return _rmsnorm(x2, final_norm_w, x.dtype).reshape(bsz, seq, d)
```

```python
import functools

import jax
import jax.numpy as jnp
from jax import lax
from jax.experimental import pallas as pl
from jax.experimental.pallas import tpu as pltpu

HEAD_DIM = 128
CHUNK = 64
GROUP = 2 * CHUNK
LANES = 128
SUBLANES = 8
CONV_W = 4
SC_CONV_W = 3
LRU_C = 8.0
EPS = 1e-6
HIST = 8
VMEM_LIMIT = 56 * 1024 * 1024
VMEM_LIMIT_RES = 61 * 1024 * 1024
SC_TN = 256

F32 = jnp.float32
BF16 = jnp.bfloat16

assert GROUP == HEAD_DIM == LANES


def _params(sem, vmem=VMEM_LIMIT):
    return pltpu.CompilerParams(dimension_semantics=sem, vmem_limit_bytes=vmem)


def _sigmoid(x):
    return jax.nn.sigmoid(x)


def _softplus(x):
    return jnp.maximum(x, 0.0) + jnp.log1p(jnp.exp(-jnp.abs(x)))


def _dot(a, b, dims=(((1,), (0,)), ((), ()))):
    return lax.dot_general(a.astype(BF16), b.astype(BF16), dims, preferred_element_type=F32)


_NT = (((1,), (1,)), ((), ()))


def _rmsnorm_kernel(x_ref, w_ref, o_ref):
    x = x_ref[...]
    ms = jnp.mean(x * x, axis=-1, keepdims=True)
    o_ref[...] = ((x * lax.rsqrt(ms + EPS)) * w_ref[...]).astype(o_ref.dtype)


def _rmsnorm(x2, w, out_dtype, tm=256):
    m, d = x2.shape
    tm = min(tm, m)
    return pl.pallas_call(
        _rmsnorm_kernel,
        grid=(m // tm,),
        in_specs=[pl.BlockSpec((tm, d), lambda i: (i, 0)),
                  pl.BlockSpec((1, d), lambda i: (0, 0))],
        out_specs=pl.BlockSpec((tm, d), lambda i: (i, 0)),
        out_shape=jax.ShapeDtypeStruct((m, d), out_dtype),
        compiler_params=_params(("arbitrary",)),
        name="rmsnorm",
    )(x2, w.reshape(1, d))


def _causal_conv(x_raw, hist, cw, width, tt, first):
    ext = jnp.concatenate([jnp.where(first, 0.0, hist[...]), x_raw], axis=0)
    hist[...] = x_raw[tt - HIST:tt, :]
    z = cw[0:1, :] * ext
    for kk in range(1, width):
        z = cw[kk:kk + 1, :] * ext + pltpu.roll(z, 1, axis=0)
    return z[HIST:HIST + tt, :]


def _rglru_kernel(xa_ref, ga_ref, cw_ref, cb_ref, wr_ref, br_ref, wi_ref, bi_ref, lam_ref,
                  o_ref, xbuf, hcar, *, tt, tc):
    t = pl.program_id(2)

    @pl.when(t == 0)
    def _():
        hcar[...] = jnp.zeros_like(hcar)

    xc = _causal_conv(xa_ref[...], xbuf, cw_ref[...], CONV_W, tt, t == 0) + cb_ref[...]

    xcb = xc.astype(BF16)
    r_parts, i_parts = [], []
    for hh in range(tc // HEAD_DIM):
        xh = xcb[:, hh * HEAD_DIM:(hh + 1) * HEAD_DIM]
        r_parts.append(jnp.dot(xh, wr_ref[hh].astype(BF16), preferred_element_type=F32))
        i_parts.append(jnp.dot(xh, wi_ref[hh].astype(BF16), preferred_element_type=F32))
    r = _sigmoid(jnp.concatenate(r_parts, axis=1) + br_ref[...])
    gi = _sigmoid(jnp.concatenate(i_parts, axis=1) + bi_ref[...])

    log_a = (-LRU_C) * r * _softplus(-lam_ref[...])
    a = jnp.exp(log_a)
    s2 = -jnp.tanh(log_a) * (a * a + 1.0)
    u = jnp.where(s2 == 0.0, 0.0, s2 * lax.rsqrt(s2)) * (gi * xc)

    rin = lax.broadcasted_iota(jnp.int32, (tt, tc), 0) & (SUBLANES - 1)
    s = 1
    while s < SUBLANES:
        keep = rin >= s
        a_sh = jnp.where(keep, pltpu.roll(a, s, axis=0), 1.0)
        u_sh = jnp.where(keep, pltpu.roll(u, s, axis=0), 0.0)
        u = a * u_sh + u
        a = a * a_sh
        s *= 2
    carry = hcar[...]
    tiles = []
    for i in range(tt // SUBLANES):
        rows = slice(i * SUBLANES, (i + 1) * SUBLANES)
        h_i = a[rows] * carry + u[rows]
        carry = h_i[SUBLANES - 1:SUBLANES, :]
        tiles.append(h_i)
    h = jnp.concatenate(tiles, axis=0)
    hcar[...] = carry

    ga = ga_ref[...]
    o_ref[...] = (h * (ga * _sigmoid(ga))).astype(o_ref.dtype)


def _rglru(p, bsz, seq, d_lru, cw, cb, w_r, b_r, w_i, b_i, lam, tt=256, tc=512):
    tt = min(tt, seq)
    tc = min(tc, d_lru)
    nt, nc = seq // tt, d_lru // tc
    hpb = tc // HEAD_DIM
    row = lambda v: v.reshape(1, d_lru)
    vec_spec = pl.BlockSpec((1, tc), lambda b, j, t: (0, j))
    return pl.pallas_call(
        functools.partial(_rglru_kernel, tt=tt, tc=tc),
        grid=(bsz, nc, nt),
        in_specs=[
            pl.BlockSpec((tt, tc), lambda b, j, t: (b * nt + t, j)),
            pl.BlockSpec((tt, tc), lambda b, j, t: (b * nt + t, nc + j)),
            pl.BlockSpec((CONV_W, tc), lambda b, j, t: (0, j)),
            vec_spec,
            pl.BlockSpec((hpb, HEAD_DIM, HEAD_DIM), lambda b, j, t: (j, 0, 0)),
            vec_spec,
            pl.BlockSpec((hpb, HEAD_DIM, HEAD_DIM), lambda b, j, t: (j, 0, 0)),
            vec_spec,
            vec_spec,
        ],
        out_specs=pl.BlockSpec((tt, tc), lambda b, j, t: (b * nt + t, j)),
        out_shape=jax.ShapeDtypeStruct((bsz * seq, d_lru), BF16),
        scratch_shapes=[pltpu.VMEM((HIST, tc), F32), pltpu.VMEM((1, tc), F32)],
        compiler_params=_params(("arbitrary", "arbitrary", "arbitrary")),
        name="rglru",
    )(p, p, cw, row(cb), w_r, row(b_r), w_i, row(b_i), row(lam))


def _norm_gates_kernel(x_ref, w_ref, wt_ref, alog_ref, dtb_ref, h_ref, g_ref, wt_bf,
                       *, nheads, tm):
    @pl.when(pl.program_id(0) == 0)
    def _():
        wt_bf[...] = wt_ref[...].astype(BF16)

    x = x_ref[...]
    ms = jnp.mean(x * x, axis=-1, keepdims=True)
    h = ((x * lax.rsqrt(ms + EPS)) * w_ref[...]).astype(BF16)
    h_ref[...] = h
    tail = lax.dot_general(h, wt_bf[...], _NT, preferred_element_type=F32)

    lane = lax.broadcasted_iota(jnp.int32, (tm, LANES), 1)
    is_a = (lane >= nheads) & (lane < 2 * nheads)
    g = jnp.where(is_a, -jnp.exp(alog_ref[...]) * _softplus(tail + dtb_ref[...]), 0.0)
    rmod = lax.broadcasted_iota(jnp.int32, (tm, LANES), 0) & (CHUNK - 1)
    s = 1
    while s < CHUNK:
        g = g + jnp.where(rmod >= s, pltpu.roll(g, s, axis=0), 0.0)
        s *= 2
    g_ref[...] = jnp.where(is_a, g, _sigmoid(tail))


def _norm_gates(x2, w, w_tail_nk, a_log, dt_bias, nheads, tm=256):
    m, d = x2.shape
    tm = min(tm, m)
    assert tm % CHUNK == 0
    pad_row = lambda v: jnp.pad(v, (nheads, LANES - 2 * nheads)).reshape(1, LANES)
    const = lambda shape: pl.BlockSpec(shape, lambda i: (0, 0))
    return pl.pallas_call(
        functools.partial(_norm_gates_kernel, nheads=nheads, tm=tm),
        grid=(m // tm,),
        in_specs=[pl.BlockSpec((tm, d), lambda i: (i, 0)), const((1, d)), const((LANES, d)),
                  const((1, LANES)), const((1, LANES))],
        out_specs=[pl.BlockSpec((tm, d), lambda i: (i, 0)),
                   pl.BlockSpec((tm, LANES), lambda i: (i, 0))],
        out_shape=[jax.ShapeDtypeStruct((m, d), BF16), jax.ShapeDtypeStruct((m, LANES), F32)],
        scratch_shapes=[pltpu.VMEM((LANES, d), BF16)],
        compiler_params=_params(("arbitrary",)),
        name="norm_gates",
    )(x2, w.reshape(1, d), w_tail_nk, pad_row(a_log), pad_row(dt_bias))


def _dn_prep_kernel(q_ref, k_ref, v_ref, gate_ref, cwq_ref, cwk_ref, cwv_ref,
                    u_ref, wq_ref, kdt_ref, attn_ref, gl_ref,
                    qbuf, kbuf, vbuf, *, tt, hg, nheads):
    hp = pl.program_id(1)
    first = pl.program_id(2) == 0

    @pl.when(first)
    def _():
        for buf in (qbuf, kbuf, vbuf):
            buf[...] = jnp.zeros_like(buf)

    def conv_silu(ref, buf, cw_ref, lo, cs):
        if lo == 0:
            ext = jnp.concatenate([buf[:, cs], ref[0:GROUP, cs]], axis=0)
        else:
            ext = ref[lo - HIST:lo + GROUP, cs]
        z = cw_ref[0:1, cs] * ext
        for kk in range(1, CONV_W):
            z = cw_ref[kk:kk + 1, cs] * ext + pltpu.roll(z, 1, axis=0)
        y = z[HIST:HIST + GROUP, :]
        return y * _sigmoid(y)

    lane = lax.broadcasted_iota(jnp.int32, (GROUP, LANES), 1)
    ii = lax.broadcasted_iota(jnp.int32, (GROUP, GROUP), 0)
    jj = lax.broadcasted_iota(jnp.int32, (GROUP, GROUP), 1)
    same_chunk = (jnp.bitwise_xor(ii, jj) & CHUNK) == 0
    causal = same_chunk & (ii >= jj)
    strict = same_chunk & (ii > jj)
    eye = jnp.where(ii == jj, 1.0, 0.0)
    top = ii < CHUNK

    chains = []
    for hh in range(hg):
        hd = hp * hg + hh
        cs = slice(hh * HEAD_DIM, (hh + 1) * HEAD_DIM)
        for g in range(tt // GROUP):
            lo = g * GROUP
            rs = slice(lo, lo + GROUP)
            q = conv_silu(q_ref, qbuf, cwq_ref, lo, cs)
            k = conv_silu(k_ref, kbuf, cwk_ref, lo, cs)
            v = conv_silu(v_ref, vbuf, cwv_ref, lo, cs)
            q = (q * lax.rsqrt(jnp.sum(q * q, axis=-1, keepdims=True) + EPS)) * (HEAD_DIM ** -0.5)
            k = k * lax.rsqrt(jnp.sum(k * k, axis=-1, keepdims=True) + EPS)
            gate = gate_ref[rs, :]
            beta = jnp.broadcast_to(
                jnp.sum(jnp.where(lane == hd, gate, 0.0), axis=-1, keepdims=True), (GROUP, LANES))
            gc = jnp.broadcast_to(
                jnp.sum(jnp.where(lane == hd + nheads, gate, 0.0), axis=-1, keepdims=True),
                (GROUP, LANES))
            chains.append(dict(cs=cs, g=g, rs=rs, q=q, k=k, v=v, beta=beta, gc=gc,
                               egc=jnp.exp(gc)))
    for ref, buf in ((q_ref, qbuf), (k_ref, kbuf), (v_ref, vbuf)):
        buf[...] = ref[tt - HIST:tt, :]

    for ch in chains:
        gcg = ch["gc"]
        ch["decay"] = jnp.exp(jnp.where(causal, gcg - gcg.T, -jnp.inf))
        ch["kb"] = ch["k"] * ch["beta"]
    for ch in chains:
        ch["kk"] = _dot(ch["kb"], ch["k"], _NT)
    for ch in chains:
        ch["qk"] = _dot(ch["q"], ch["k"], _NT)
    for ch in chains:
        ch["p"] = -jnp.where(strict, ch["kk"] * ch["decay"], 0.0)
        ch["t"] = eye + ch["p"]
    pw = 2
    while pw < CHUNK:
        for ch in chains:
            ch["p"] = _dot(ch["p"], ch["p"])
        for ch in chains:
            ch["t"] = ch["t"] + _dot(ch["p"], ch["t"])
        pw *= 2
    for ch in chains:
        rhs = jnp.concatenate([ch["v"] * ch["beta"], ch["kb"] * ch["egc"]], axis=1)
        ch["sol"] = _dot(ch["t"], rhs)

    for ch in chains:
        cs, g, rs, gcg = ch["cs"], ch["g"], ch["rs"], ch["gc"]
        lo = g * GROUP
        u = ch["sol"][:, 0:HEAD_DIM]
        w = ch["sol"][:, HEAD_DIM:2 * HEAD_DIM].astype(BF16)
        attn = ch["qk"] * ch["decay"]
        q_dec = (ch["q"] * ch["egc"]).astype(BF16)
        gl0 = gcg[CHUNK - 1:CHUNK, :]
        gl1 = gcg[GROUP - 1:GROUP, :]
        k_dec = ch["k"] * jnp.exp(jnp.where(top, gl0, gl1) - gcg)

        u_ref[rs, cs] = u
        for c in range(2):
            cr = slice(c * CHUNK, (c + 1) * CHUNK)
            wq_ref[2 * lo + c * GROUP:2 * lo + c * GROUP + CHUNK, cs] = w[cr]
            wq_ref[2 * lo + c * GROUP + CHUNK:2 * lo + (c + 1) * GROUP, cs] = q_dec[cr]
        kdt_ref[rs, cs] = k_dec.T.astype(BF16)
        attn_ref[rs, cs] = attn.astype(BF16)
        gl_ref[2 * g:2 * g + 1, cs] = jnp.exp(gl0)
        gl_ref[2 * g + 1:2 * g + 2, cs] = jnp.exp(gl1)


def _dn_scan_kernel(u_ref, wq_ref, kdt_ref, attn_ref, gl_ref, z_ref, nw_ref, o_ref, s_ref,
                    *, tt, hg):
    @pl.when(pl.program_id(2) == 0)
    def _():
        s_ref[...] = jnp.zeros_like(s_ref)

    nw = nw_ref[...]
    zeros = jnp.zeros((CHUNK, HEAD_DIM), F32)
    cols = [slice(hh * HEAD_DIM, (hh + 1) * HEAD_DIM) for hh in range(hg)]
    s_mats = [s_ref[hh] for hh in range(hg)]
    for g in range(tt // GROUP):
        lo = g * GROUP
        rs = slice(lo, lo + GROUP)
        v_prev = [zeros] * hg
        outs = [[] for _ in range(hg)]
        for c in range(2):
            r = [_dot(wq_ref[2 * lo + c * GROUP:2 * lo + (c + 1) * GROUP, cs], s_mats[hh])
                 for hh, cs in enumerate(cols)]
            v_new = [u_ref[lo + c * CHUNK:lo + (c + 1) * CHUNK, cs] - r[hh][0:CHUNK]
                     for hh, cs in enumerate(cols)]
            for hh, cs in enumerate(cols):
                v_upd = jnp.concatenate([zeros, v_new[hh]] if c else [v_new[hh], zeros], axis=0)
                s_mats[hh] = (s_mats[hh] * gl_ref[2 * g + c:2 * g + c + 1, cs]
                              + _dot(kdt_ref[rs, cs], v_upd))
            for hh, cs in enumerate(cols):
                v_att = jnp.concatenate([v_prev[hh], v_new[hh]] if c else [v_new[hh], zeros], axis=0)
                outs[hh].append(r[hh][CHUNK:GROUP]
                                + _dot(attn_ref[lo + c * CHUNK:lo + (c + 1) * CHUNK, cs], v_att))
            v_prev = v_new
        for hh, cs in enumerate(cols):
            o = jnp.concatenate(outs[hh], axis=0)
            ms = jnp.mean(o * o, axis=-1, keepdims=True)
            z = z_ref[rs, cs]
            o_ref[rs, cs] = (((o * lax.rsqrt(ms + EPS)) * nw) * (z * _sigmoid(z))).astype(o_ref.dtype)
    for hh in range(hg):
        s_ref[hh] = s_mats[hh]


def _deltanet(p, gates, bsz, seq, d_lru, d_dn, conv_w3, layer, norm_w, tt=512, hg_prep=4,
              hg_scan=8):
    nheads = d_dn // HEAD_DIM
    m = bsz * seq
    tt = min(tt, seq)
    hg_prep = min(hg_prep, nheads)
    hg_scan = min(hg_scan, nheads)
    nt = seq // tt
    base = 2 * d_lru // HEAD_DIM
    ncb = 2 * tt // GROUP

    def prep_cols(off):
        return pl.BlockSpec((tt, hg_prep * HEAD_DIM),
                            lambda b, h, t: (b * nt + t, (base + off * nheads) // hg_prep + h))

    def prep_cw(off):
        return pl.BlockSpec((None, CONV_W, hg_prep * HEAD_DIM),
                            lambda b, h, t: (layer, 0, off * nheads // hg_prep + h))

    assert base % hg_prep == 0 and nheads % hg_prep == 0 and nheads % hg_scan == 0
    wide = hg_prep * HEAD_DIM
    row_blk = lambda rows: pl.BlockSpec((rows, wide), lambda b, h, t: (b * nt + t, h))
    u, wq, kdt, attn, gl = pl.pallas_call(
        functools.partial(_dn_prep_kernel, tt=tt, hg=hg_prep, nheads=nheads),
        grid=(bsz, nheads // hg_prep, nt),
        in_specs=[prep_cols(0), prep_cols(1), prep_cols(2),
                  pl.BlockSpec((tt, LANES), lambda b, h, t: (b * nt + t, 0)),
                  prep_cw(0), prep_cw(1), prep_cw(2)],
        out_specs=[row_blk(tt), row_blk(2 * tt), row_blk(tt), row_blk(tt), row_blk(ncb)],
        out_shape=[jax.ShapeDtypeStruct((m, d_dn), F32),
                   jax.ShapeDtypeStruct((2 * m, d_dn), BF16),
                   jax.ShapeDtypeStruct((m, d_dn), BF16),
                   jax.ShapeDtypeStruct((m, d_dn), BF16),
                   jax.ShapeDtypeStruct((m // CHUNK, d_dn), F32)],
        scratch_shapes=[pltpu.VMEM((HIST, wide), F32)] * 3,
        compiler_params=_params(("arbitrary", "arbitrary", "arbitrary")),
        name="dn_prep",
    )(p, p, p, gates, conv_w3, conv_w3, conv_w3)

    wide = hg_scan * HEAD_DIM
    row_blk = lambda rows: pl.BlockSpec((rows, wide), lambda b, h, t: (b * nt + t, h))
    z_off = (base + 3 * nheads) // hg_scan
    assert (base + 3 * nheads) % hg_scan == 0
    return pl.pallas_call(
        functools.partial(_dn_scan_kernel, tt=tt, hg=hg_scan),
        grid=(bsz, nheads // hg_scan, nt),
        in_specs=[row_blk(tt), row_blk(2 * tt), row_blk(tt), row_blk(tt), row_blk(ncb),
                  pl.BlockSpec((tt, wide), lambda b, h, t: (b * nt + t, z_off + h)),
                  pl.BlockSpec((1, HEAD_DIM), lambda b, h, t: (0, 0))],
        out_specs=row_blk(tt),
        out_shape=jax.ShapeDtypeStruct((m, d_dn), BF16),
        scratch_shapes=[pltpu.VMEM((hg_scan, HEAD_DIM, HEAD_DIM), F32)],
        compiler_params=_params(("arbitrary", "arbitrary", "arbitrary")),
        name="dn_scan",
    )(u, wq, kdt, attn, gl, p, norm_w.reshape(1, HEAD_DIM))


def _stream_kernel(*refs, n_a, n_w, w_is_nk, shortconv, has_res, mi, nj, nb, tm, tn, ks, layer,
                   blocks_per_seq):
    a_refs, w_hbm = refs[:n_a], refs[n_a]
    rest = list(refs[n_a + 1:])
    cw_ref = rest.pop(0) if shortconv else None
    r_ref = rest.pop(0) if has_res else None
    o_ref, wbf0, wbf1, stage, sem = rest[:5]
    hist = rest[5] if shortconv else None
    wbf = (wbf0, wbf1)
    j = pl.program_id(0)
    i = pl.program_id(1)
    lin = j * mi + i
    n_steps = nj * mi

    def slab_copies(l, slot):
        jb = lax.rem(lax.div(l, mi), nj)
        r = lax.rem(l, mi)
        out = []
        for n in range(n_w):
            if w_is_nk:
                src = w_hbm.at[layer, pl.ds(jb * tn + r * ks, ks), :]
            else:
                src = w_hbm.at[layer, pl.ds(r * ks, ks), pl.ds((n * nb + jb) * tn, tn)]
            out.append(pltpu.make_async_copy(src, stage.at[slot, n], sem.at[slot, n]))
        return out

    def cast_slab(slot, r, dst):
        rows = pl.ds(pl.multiple_of(r * ks, ks), ks)
        for n in range(n_w):
            if w_is_nk:
                dst[rows, :] = stage[slot, n].astype(BF16)
            else:
                for h in range(tn // LANES):
                    at = (h * n_w + n) * LANES
                    dst[rows, at:at + LANES] = stage[slot, n, :, h * LANES:(h + 1) * LANES].astype(BF16)

    @pl.when(lin == 0)
    def _():
        for r in range(mi):
            for c in slab_copies(r, r % 2):
                c.start()
            for c in slab_copies(r, r % 2):
                c.wait()
            cast_slab(r % 2, r, wbf[0])
        for c in slab_copies(mi, 0):
            c.start()

    def body(cur, oth):
        slot = lax.rem(i, 2)
        for c in slab_copies(lin + mi, slot):
            c.wait()
        for c in slab_copies(lin + mi + 1, 1 - slot):
            c.start()
        cast_slab(slot, i, oth)
        proj = None
        off = 0
        for a_ref in a_refs:
            kk = a_ref.shape[1]
            if w_is_nk:
                part = lax.dot_general(a_ref[...], cur[:, off:off + kk], _NT,
                                       preferred_element_type=F32)
            else:
                part = jnp.dot(a_ref[...], cur[off:off + kk, :], preferred_element_type=F32)
            proj = part if proj is None else proj + part
            off += kk
        if has_res:
            proj = proj + r_ref[...]
        if shortconv:
            first = lax.rem(i, blocks_per_seq) == 0
            for h in range(tn // LANES):
                xin, gate_b, gate_c, z = (
                    proj[:, (h * n_w + n) * LANES:(h * n_w + n + 1) * LANES] for n in range(4))
                cols = pl.ds(h * LANES, LANES)
                conv = _causal_conv(gate_c * xin, hist.at[:, cols], cw_ref[:, cols], SC_CONV_W,
                                    tm, first)
                o_ref[:, cols] = ((gate_b * conv) * (z * _sigmoid(z))).astype(o_ref.dtype)
        else:
            o_ref[...] = proj.astype(o_ref.dtype)

    @pl.when(lax.rem(j, 2) == 0)
    def _():
        body(wbf[0], wbf[1])

    @pl.when(lax.rem(j, 2) == 1)
    def _():
        body(wbf[1], wbf[0])

    @pl.when(lin == n_steps - 1)
    def _():
        for c in slab_copies(lin + mi + 1, 1 - lax.rem(i, 2)):
            c.wait()


def _stream_matmul(a_parts, w3, layer, n_cols, *, w_is_nk=False, conv_w3=None, seq=None, res=None,
                   out_dtype=F32, tm=1024, tn=1024, name="stream_matmul"):
    m = a_parts[0].shape[0]
    k = sum(a.shape[1] for a in a_parts)
    shortconv = conv_w3 is not None
    n_w = 4 if shortconv else 1
    tm = min(tm, seq if shortconv else m)
    if (m // tm) % 2:
        tm //= 2
    tn = max(t for t in range(LANES, min(tn, n_cols) + 1, LANES) if n_cols % t == 0)
    mi, nj = m // tm, n_cols // tn
    assert m % tm == 0 and mi % 2 == 0 and not (shortconv and w_is_nk)
    if w_is_nk:
        ks = tn // mi
        wbf_shape, stage_shape = (tn, k), (2, 1, ks, k)
        assert tn % mi == 0 and ks % 16 == 0
    else:
        ks = k // mi
        wbf_shape, stage_shape = (k, n_w * tn), (2, n_w, ks, tn)
        assert k % mi == 0 and ks % 16 == 0
    in_specs = [pl.BlockSpec((tm, a.shape[1]), lambda j, i: (i, 0)) for a in a_parts]
    in_specs.append(pl.BlockSpec(memory_space=pl.ANY))
    args = list(a_parts) + [w3]
    scratch = [pltpu.VMEM(wbf_shape, BF16), pltpu.VMEM(wbf_shape, BF16),
               pltpu.VMEM(stage_shape, F32), pltpu.SemaphoreType.DMA((2, n_w))]
    if shortconv:
        in_specs.append(pl.BlockSpec((None, SC_CONV_W, tn), lambda j, i: (layer, 0, j)))
        args.append(conv_w3)
        scratch.append(pltpu.VMEM((HIST, tn), F32))
    if res is not None:
        in_specs.append(pl.BlockSpec((tm, tn), lambda j, i: (i, j)))
        args.append(res)
    return pl.pallas_call(
        functools.partial(_stream_kernel, n_a=len(a_parts), n_w=n_w, w_is_nk=w_is_nk,
                          shortconv=shortconv, has_res=res is not None, mi=mi,
                          nj=nj, nb=nj, tm=tm, tn=tn, ks=ks, layer=layer,
                          blocks_per_seq=(seq // tm) if shortconv else 1),
        grid=(nj, mi),
        in_specs=in_specs,
        out_specs=pl.BlockSpec((tm, tn), lambda j, i: (i, j)),
        out_shape=jax.ShapeDtypeStruct((m, n_cols), out_dtype),
        scratch_shapes=scratch,
        compiler_params=_params(("arbitrary", "arbitrary"),
                                VMEM_LIMIT_RES if res is not None else VMEM_LIMIT),
        name=name,
    )(*args)


def kernel(x, even_norm_w, even_w_in, lru_conv_w, lru_conv_b, lru_w_r, lru_b_r, lru_w_i, lru_b_i,
           lru_lambda, dn_conv_w, dn_a_log, dn_dt_bias, dn_norm_w, even_w_out, odd_norm_w,
           odd_w_in, odd_conv_w, odd_w_out, final_norm_w):
    bsz, seq, d = x.shape
    x2 = x.reshape(bsz * seq, d)
    d_lru = lru_lambda.shape[1]
    d_dn = dn_conv_w.shape[2] // 3
    nheads = d_dn // HEAD_DIM
    n_main = 2 * d_lru + 4 * d_dn
    depth = even_norm_w.shape[0] + odd_norm_w.shape[0]
    for layer in range(depth):
        j = layer // 2
        if layer % 2 == 0:
            w_in_nk = jnp.swapaxes(even_w_in, 1, 2)
            w_tail = jnp.pad(w_in_nk[j, n_main:, :], ((0, LANES - 2 * nheads), (0, 0)))
            h, gates = _norm_gates(x2, even_norm_w[j], w_tail, dn_a_log[j], dn_dt_bias[j], nheads)
            p = _stream_matmul([h], w_in_nk, j, n_main, w_is_nk=True, name="in_proj0")
            ya = _rglru(p, bsz, seq, d_lru, lru_conv_w[j], lru_conv_b[j], lru_w_r[j], lru_b_r[j],
                        lru_w_i[j], lru_b_i[j], lru_lambda[j])
            yb = _deltanet(p, gates, bsz, seq, d_lru, d_dn, dn_conv_w, j, dn_norm_w[j])
            x2 = _stream_matmul([ya, yb], even_w_out, j, d, res=x2, name="out_proj0")
        else:
            h = _rmsnorm(x2, odd_norm_w[j], BF16)
            y = _stream_matmul([h], odd_w_in, j, odd_conv_w.shape[2], conv_w3=odd_conv_w, seq=seq,
                               out_dtype=BF16, tn=SC_TN, name="shortconv")
            x2 = _stream_matmul([y], odd_w_out, j, d, res=x2, name="out_proj1")
    return _rmsnorm(x2, final_norm_w, x.dtype).reshape(bsz, seq, d)
```

```python
import functools

import jax
import jax.numpy as jnp
from jax import lax
from jax.experimental import pallas as pl
from jax.experimental.pallas import tpu as pltpu

HEAD_DIM = 128
CHUNK = 64
GROUP = 2 * CHUNK
LANES = 128
SUBLANES = 8
CONV_W = 4
SC_CONV_W = 3
LRU_C = 8.0
EPS = 1e-6
HIST = 8
VMEM_LIMIT = 56 * 1024 * 1024
VMEM_LIMIT_RES = 61 * 1024 * 1024
SC_TN = 256

F32 = jnp.float32
BF16 = jnp.bfloat16

assert GROUP == HEAD_DIM == LANES


def _params(sem, vmem=VMEM_LIMIT):
    return pltpu.CompilerParams(dimension_semantics=sem, vmem_limit_bytes=vmem)


def _sigmoid(x):
    return jax.nn.sigmoid(x)


def _softplus(x):
    return jnp.maximum(x, 0.0) + jnp.log1p(jnp.exp(-jnp.abs(x)))


def _dot(a, b, dims=(((1,), (0,)), ((), ()))):
    return lax.dot_general(a.astype(BF16), b.astype(BF16), dims, preferred_element_type=F32)


_NT = (((1,), (1,)), ((), ()))


def _rmsnorm_kernel(x_ref, w_ref, o_ref):
    x = x_ref[...]
    ms = jnp.mean(x * x, axis=-1, keepdims=True)
    o_ref[...] = ((x * lax.rsqrt(ms + EPS)) * w_ref[...]).astype(o_ref.dtype)


def _rmsnorm(x2, w, out_dtype, tm=256):
    m, d = x2.shape
    tm = min(tm, m)
    return pl.pallas_call(
        _rmsnorm_kernel,
        grid=(m // tm,),
        in_specs=[pl.BlockSpec((tm, d), lambda i: (i, 0)),
                  pl.BlockSpec((1, d), lambda i: (0, 0))],
        out_specs=pl.BlockSpec((tm, d), lambda i: (i, 0)),
        out_shape=jax.ShapeDtypeStruct((m, d), out_dtype),
        compiler_params=_params(("arbitrary",)),
        name="rmsnorm",
    )(x2, w.reshape(1, d))


def _causal_conv(x_raw, hist, cw, width, tt, first):
    ext = jnp.concatenate([jnp.where(first, 0.0, hist[...]), x_raw], axis=0)
    hist[...] = x_raw[tt - HIST:tt, :]
    z = cw[0:1, :] * ext
    for kk in range(1, width):
        z = cw[kk:kk + 1, :] * ext + pltpu.roll(z, 1, axis=0)
    return z[HIST:HIST + tt, :]


def _rglru_kernel(xa_ref, ga_ref, cw_ref, cb_ref, wr_ref, br_ref, wi_ref, bi_ref, lam_ref,
                  o_ref, xbuf, hcar, *, tt, tc):
    t = pl.program_id(2)

    @pl.when(t == 0)
    def _():
        hcar[...] = jnp.zeros_like(hcar)

    xc = _causal_conv(xa_ref[...], xbuf, cw_ref[...], CONV_W, tt, t == 0) + cb_ref[...]

    xcb = xc.astype(BF16)
    r_parts, i_parts = [], []
    for hh in range(tc // HEAD_DIM):
        xh = xcb[:, hh * HEAD_DIM:(hh + 1) * HEAD_DIM]
        r_parts.append(jnp.dot(xh, wr_ref[hh].astype(BF16), preferred_element_type=F32))
        i_parts.append(jnp.dot(xh, wi_ref[hh].astype(BF16), preferred_element_type=F32))
    r = _sigmoid(jnp.concatenate(r_parts, axis=1) + br_ref[...])
    gi = _sigmoid(jnp.concatenate(i_parts, axis=1) + bi_ref[...])

    log_a = (-LRU_C) * r * _softplus(-lam_ref[...])
    a = jnp.exp(log_a)
    s2 = -jnp.tanh(log_a) * (a * a + 1.0)
    u = jnp.where(s2 == 0.0, 0.0, s2 * lax.rsqrt(s2)) * (gi * xc)

    rin = lax.broadcasted_iota(jnp.int32, (tt, tc), 0) & (SUBLANES - 1)
    s = 1
    while s < SUBLANES:
        keep = rin >= s
        a_sh = jnp.where(keep, pltpu.roll(a, s, axis=0), 1.0)
        u_sh = jnp.where(keep, pltpu.roll(u, s, axis=0), 0.0)
        u = a * u_sh + u
        a = a * a_sh
        s *= 2
    carry = hcar[...]
    tiles = []
    for i in range(tt // SUBLANES):
        rows = slice(i * SUBLANES, (i + 1) * SUBLANES)
        h_i = a[rows] * carry + u[rows]
        carry = h_i[SUBLANES - 1:SUBLANES, :]
        tiles.append(h_i)
    h = jnp.concatenate(tiles, axis=0)
    hcar[...] = carry

    ga = ga_ref[...]
    o_ref[...] = (h * (ga * _sigmoid(ga))).astype(o_ref.dtype)


def _rglru(p, bsz, seq, d_lru, cw, cb, w_r, b_r, w_i, b_i, lam, tt=256, tc=512):
    tt = min(tt, seq)
    tc = min(tc, d_lru)
    nt, nc = seq // tt, d_lru // tc
    hpb = tc // HEAD_DIM
    row = lambda v: v.reshape(1, d_lru)
    vec_spec = pl.BlockSpec((1, tc), lambda b, j, t: (0, j))
    return pl.pallas_call(
        functools.partial(_rglru_kernel, tt=tt, tc=tc),
        grid=(bsz, nc, nt),
        in_specs=[
            pl.BlockSpec((tt, tc), lambda b, j, t: (b * nt + t, j)),
            pl.BlockSpec((tt, tc), lambda b, j, t: (b * nt + t, nc + j)),
            pl.BlockSpec((CONV_W, tc), lambda b, j, t: (0, j)),
            vec_spec,
            pl.BlockSpec((hpb, HEAD_DIM, HEAD_DIM), lambda b, j, t: (j, 0, 0)),
            vec_spec,
            pl.BlockSpec((hpb, HEAD_DIM, HEAD_DIM), lambda b, j, t: (j, 0, 0)),
            vec_spec,
            vec_spec,
        ],
        out_specs=pl.BlockSpec((tt, tc), lambda b, j, t: (b * nt + t, j)),
        out_shape=jax.ShapeDtypeStruct((bsz * seq, d_lru), BF16),
        scratch_shapes=[pltpu.VMEM((HIST, tc), F32), pltpu.VMEM((1, tc), F32)],
        compiler_params=_params(("arbitrary", "arbitrary", "arbitrary")),
        name="rglru",
    )(p, p, cw, row(cb), w_r, row(b_r), w_i, row(b_i), row(lam))


def _norm_gates_kernel(x_ref, w_ref, wt_ref, alog_ref, dtb_ref, h_ref, g_ref, wt_bf,
                       *, nheads, tm):
    @pl.when(pl.program_id(0) == 0)
    def _():
        wt_bf[...] = wt_ref[...].astype(BF16)

    x = x_ref[...]
    ms = jnp.mean(x * x, axis=-1, keepdims=True)
    h = ((x * lax.rsqrt(ms + EPS)) * w_ref[...]).astype(BF16)
    h_ref[...] = h
    tail = lax.dot_general(h, wt_bf[...], _NT, preferred_element_type=F32)

    lane = lax.broadcasted_iota(jnp.int32, (tm, LANES), 1)
    is_a = (lane >= nheads) & (lane < 2 * nheads)
    g = jnp.where(is_a, -jnp.exp(alog_ref[...]) * _softplus(tail + dtb_ref[...]), 0.0)
    rmod = lax.broadcasted_iota(jnp.int32, (tm, LANES), 0) & (CHUNK - 1)
    s = 1
    while s < CHUNK:
        g = g + jnp.where(rmod >= s, pltpu.roll(g, s, axis=0), 0.0)
        s *= 2
    g_ref[...] = jnp.where(is_a, g, _sigmoid(tail))


def _norm_gates(x2, w, w_tail_nk, a_log, dt_bias, nheads, tm=256):
    m, d = x2.shape
    tm = min(tm, m)
    assert tm % CHUNK == 0
    pad_row = lambda v: jnp.pad(v, (nheads, LANES - 2 * nheads)).reshape(1, LANES)
    const = lambda shape: pl.BlockSpec(shape, lambda i: (0, 0))
    return pl.pallas_call(
        functools.partial(_norm_gates_kernel, nheads=nheads, tm=tm),
        grid=(m // tm,),
        in_specs=[pl.BlockSpec((tm, d), lambda i: (i, 0)), const((1, d)), const((LANES, d)),
                  const((1, LANES)), const((1, LANES))],
        out_specs=[pl.BlockSpec((tm, d), lambda i: (i, 0)),
                   pl.BlockSpec((tm, LANES), lambda i: (i, 0))],
        out_shape=[jax.ShapeDtypeStruct((m, d), BF16), jax.ShapeDtypeStruct((m, LANES), F32)],
        scratch_shapes=[pltpu.VMEM((LANES, d), BF16)],
        compiler_params=_params(("arbitrary",)),
        name="norm_gates",
    )(x2, w.reshape(1, d), w_tail_nk, pad_row(a_log), pad_row(dt_bias))


def _dn_prep_kernel(q_ref, k_ref, v_ref, gate_ref, cwq_ref, cwk_ref, cwv_ref,
                    u_ref, wq_ref, kdt_ref, attn_ref, gl_ref,
                    qbuf, kbuf, vbuf, *, tt, hg, nheads):
    hp = pl.program_id(1)
    first = pl.program_id(2) == 0

    @pl.when(first)
    def _():
        for buf in (qbuf, kbuf, vbuf):
            buf[...] = jnp.zeros_like(buf)

    def conv_silu(ref, buf, cw_ref, lo, cs):
        if lo == 0:
            ext = jnp.concatenate([buf[:, cs], ref[0:GROUP, cs]], axis=0)
        else:
            ext = ref[lo - HIST:lo + GROUP, cs]
        z = cw_ref[0:1, cs] * ext
        for kk in range(1, CONV_W):
            z = cw_ref[kk:kk + 1, cs] * ext + pltpu.roll(z, 1, axis=0)
        y = z[HIST:HIST + GROUP, :]
        return y * _sigmoid(y)

    lane = lax.broadcasted_iota(jnp.int32, (GROUP, LANES), 1)
    ii = lax.broadcasted_iota(jnp.int32, (GROUP, GROUP), 0)
    jj = lax.broadcasted_iota(jnp.int32, (GROUP, GROUP), 1)
    same_chunk = (jnp.bitwise_xor(ii, jj) & CHUNK) == 0
    causal = same_chunk & (ii >= jj)
    strict = same_chunk & (ii > jj)
    eye = jnp.where(ii == jj, 1.0, 0.0)
    top = ii < CHUNK

    chains = []
    for hh in range(hg):
        hd = hp * hg + hh
        cs = slice(hh * HEAD_DIM, (hh + 1) * HEAD_DIM)
        for g in range(tt // GROUP):
            lo = g * GROUP
            rs = slice(lo, lo + GROUP)
            q = conv_silu(q_ref, qbuf, cwq_ref, lo, cs)
            k = conv_silu(k_ref, kbuf, cwk_ref, lo, cs)
            v = conv_silu(v_ref, vbuf, cwv_ref, lo, cs)
            q = (q * lax.rsqrt(jnp.sum(q * q, axis=-1, keepdims=True) + EPS)) * (HEAD_DIM ** -0.5)
            k = k * lax.rsqrt(jnp.sum(k * k, axis=-1, keepdims=True) + EPS)
            gate = gate_ref[rs, :]
            beta = jnp.broadcast_to(
                jnp.sum(jnp.where(lane == hd, gate, 0.0), axis=-1, keepdims=True), (GROUP, LANES))
            gc = jnp.broadcast_to(
                jnp.sum(jnp.where(lane == hd + nheads, gate, 0.0), axis=-1, keepdims=True),
                (GROUP, LANES))
            chains.append(dict(cs=cs, g=g, rs=rs, q=q, k=k, v=v, beta=beta, gc=gc,
                               egc=jnp.exp(gc)))
    for ref, buf in ((q_ref, qbuf), (k_ref, kbuf), (v_ref, vbuf)):
        buf[...] = ref[tt - HIST:tt, :]

    for ch in chains:
        gcg = ch["gc"]
        ch["decay"] = jnp.exp(jnp.where(causal, gcg - gcg.T, -jnp.inf))
        ch["kb"] = ch["k"] * ch["beta"]
    for ch in chains:
        ch["kk"] = _dot(ch["kb"], ch["k"], _NT)
    for ch in chains:
        ch["qk"] = _dot(ch["q"], ch["k"], _NT)
    for ch in chains:
        ch["p"] = -jnp.where(strict, ch["kk"] * ch["decay"], 0.0)
        ch["t"] = eye + ch["p"]
    pw = 2
    while pw < CHUNK:
        for ch in chains:
            ch["p"] = _dot(ch["p"], ch["p"])
        for ch in chains:
            ch["t"] = ch["t"] + _dot(ch["p"], ch["t"])
        pw *= 2
    for ch in chains:
        rhs = jnp.concatenate([ch["v"] * ch["beta"], ch["kb"] * ch["egc"]], axis=1)
        ch["sol"] = _dot(ch["t"], rhs)

    for ch in chains:
        cs, g, rs, gcg = ch["cs"], ch["g"], ch["rs"], ch["gc"]
        lo = g * GROUP
        u = ch["sol"][:, 0:HEAD_DIM]
        w = ch["sol"][:, HEAD_DIM:2 * HEAD_DIM].astype(BF16)
        attn = ch["qk"] * ch["decay"]
        q_dec = (ch["q"] * ch["egc"]).astype(BF16)
        gl0 = gcg[CHUNK - 1:CHUNK, :]
        gl1 = gcg[GROUP - 1:GROUP, :]
        k_dec = ch["k"] * jnp.exp(jnp.where(top, gl0, gl1) - gcg)

        u_ref[rs, cs] = u
        for c in range(2):
            cr = slice(c * CHUNK, (c + 1) * CHUNK)
            wq_ref[2 * lo + c * GROUP:2 * lo + c * GROUP + CHUNK, cs] = w[cr]
            wq_ref[2 * lo + c * GROUP + CHUNK:2 * lo + (c + 1) * GROUP, cs] = q_dec[cr]
        kdt_ref[rs, cs] = k_dec.T.astype(BF16)
        attn_ref[rs, cs] = attn.astype(BF16)
        gl_ref[2 * g:2 * g + 1, cs] = jnp.exp(gl0)
        gl_ref[2 * g + 1:2 * g + 2, cs] = jnp.exp(gl1)


def _dn_scan_kernel(u_ref, wq_ref, kdt_ref, attn_ref, gl_ref, z_ref, nw_ref, o_ref, s_ref,
                    *, tt, hg):
    @pl.when(pl.program_id(2) == 0)
    def _():
        s_ref[...] = jnp.zeros_like(s_ref)

    nw = nw_ref[...]
    zeros = jnp.zeros((CHUNK, HEAD_DIM), F32)
    cols = [slice(hh * HEAD_DIM, (hh + 1) * HEAD_DIM) for hh in range(hg)]
    s_mats = [s_ref[hh] for hh in range(hg)]
    for g in range(tt // GROUP):
        lo = g * GROUP
        rs = slice(lo, lo + GROUP)
        v_prev = [zeros] * hg
        outs = [[] for _ in range(hg)]
        for c in range(2):
            r = [_dot(wq_ref[2 * lo + c * GROUP:2 * lo + (c + 1) * GROUP, cs], s_mats[hh])
                 for hh, cs in enumerate(cols)]
            v_new = [u_ref[lo + c * CHUNK:lo + (c + 1) * CHUNK, cs] - r[hh][0:CHUNK]
                     for hh, cs in enumerate(cols)]
            for hh, cs in enumerate(cols):
                v_upd = jnp.concatenate([zeros, v_new[hh]] if c else [v_new[hh], zeros], axis=0)
                s_mats[hh] = (s_mats[hh] * gl_ref[2 * g + c:2 * g + c + 1, cs]
                              + _dot(kdt_ref[rs, cs], v_upd))
            for hh, cs in enumerate(cols):
                v_att = jnp.concatenate([v_prev[hh], v_new[hh]] if c else [v_new[hh], zeros], axis=0)
                outs[hh].append(r[hh][CHUNK:GROUP]
                                + _dot(attn_ref[lo + c * CHUNK:lo + (c + 1) * CHUNK, cs], v_att))
            v_prev = v_new
        for hh, cs in enumerate(cols):
            o = jnp.concatenate(outs[hh], axis=0)
            ms = jnp.mean(o * o, axis=-1, keepdims=True)
            z = z_ref[rs, cs]
            o_ref[rs, cs] = (((o * lax.rsqrt(ms + EPS)) * nw) * (z * _sigmoid(z))).astype(o_ref.dtype)
    for hh in range(hg):
        s_ref[hh] = s_mats[hh]


def _deltanet(p, gates, bsz, seq, d_lru, d_dn, conv_w3, layer, norm_w, tt=512, hg_prep=4,
              hg_scan=8):
    nheads = d_dn // HEAD_DIM
    m = bsz * seq
    tt = min(tt, seq)
    hg_prep = min(hg_prep, nheads)
    hg_scan = min(hg_scan, nheads)
    nt = seq // tt
    base = 2 * d_lru // HEAD_DIM
    ncb = 2 * tt // GROUP

    def prep_cols(off):
        return pl.BlockSpec((tt, hg_prep * HEAD_DIM),
                            lambda b, h, t: (b * nt + t, (base + off * nheads) // hg_prep + h))

    def prep_cw(off):
        return pl.BlockSpec((None, CONV_W, hg_prep * HEAD_DIM),
                            lambda b, h, t: (layer, 0, off * nheads // hg_prep + h))

    assert base % hg_prep == 0 and nheads % hg_prep == 0 and nheads % hg_scan == 0
    wide = hg_prep * HEAD_DIM
    row_blk = lambda rows: pl.BlockSpec((rows, wide), lambda b, h, t: (b * nt + t, h))
    u, wq, kdt, attn, gl = pl.pallas_call(
        functools.partial(_dn_prep_kernel, tt=tt, hg=hg_prep, nheads=nheads),
        grid=(bsz, nheads // hg_prep, nt),
        in_specs=[prep_cols(0), prep_cols(1), prep_cols(2),
                  pl.BlockSpec((tt, LANES), lambda b, h, t: (b * nt + t, 0)),
                  prep_cw(0), prep_cw(1), prep_cw(2)],
        out_specs=[row_blk(tt), row_blk(2 * tt), row_blk(tt), row_blk(tt), row_blk(ncb)],
        out_shape=[jax.ShapeDtypeStruct((m, d_dn), F32),
                   jax.ShapeDtypeStruct((2 * m, d_dn), BF16),
                   jax.ShapeDtypeStruct((m, d_dn), BF16),
                   jax.ShapeDtypeStruct((m, d_dn), BF16),
                   jax.ShapeDtypeStruct((m // CHUNK, d_dn), F32)],
        scratch_shapes=[pltpu.VMEM((HIST, wide), F32)] * 3,
        compiler_params=_params(("arbitrary", "arbitrary", "arbitrary")),
        name="dn_prep",
    )(p, p, p, gates, conv_w3, conv_w3, conv_w3)

    wide = hg_scan * HEAD_DIM
    row_blk = lambda rows: pl.BlockSpec((rows, wide), lambda b, h, t: (b * nt + t, h))
    z_off = (base + 3 * nheads) // hg_scan
    assert (base + 3 * nheads) % hg_scan == 0
    return pl.pallas_call(
        functools.partial(_dn_scan_kernel, tt=tt, hg=hg_scan),
        grid=(bsz, nheads // hg_scan, nt),
        in_specs=[row_blk(tt), row_blk(2 * tt), row_blk(tt), row_blk(tt), row_blk(ncb),
                  pl.BlockSpec((tt, wide), lambda b, h, t: (b * nt + t, z_off + h)),
                  pl.BlockSpec((1, HEAD_DIM), lambda b, h, t: (0, 0))],
        out_specs=row_blk(tt),
        out_shape=jax.ShapeDtypeStruct((m, d_dn), BF16),
        scratch_shapes=[pltpu.VMEM((hg_scan, HEAD_DIM, HEAD_DIM), F32)],
        compiler_params=_params(("arbitrary", "arbitrary", "arbitrary")),
        name="dn_scan",
    )(u, wq, kdt, attn, gl, p, norm_w.reshape(1, HEAD_DIM))


def _stream_kernel(*refs, n_a, n_w, w_is_nk, shortconv, has_res, mi, nj, nb, tm, tn, ks, layer,
                   blocks_per_seq):
    a_refs, w_hbm = refs[:n_a], refs[n_a]
    rest = list(refs[n_a + 1:])
    cw_ref = rest.pop(0) if shortconv else None
    r_ref = rest.pop(0) if has_res else None
    o_ref, wbf0, wbf1, stage, sem = rest[:5]
    hist = rest[5] if shortconv else None
    wbf = (wbf0, wbf1)
    j = pl.program_id(0)
    i = pl.program_id(1)
    lin = j * mi + i
    n_steps = nj * mi

    def slab_copies(l, slot):
        jb = lax.rem(lax.div(l, mi), nj)
        r = lax.rem(l, mi)
        out = []
        for n in range(n_w):
            if w_is_nk:
                src = w_hbm.at[layer, pl.ds(jb * tn + r * ks, ks), :]
            else:
                src = w_hbm.at[layer, pl.ds(r * ks, ks), pl.ds((n * nb + jb) * tn, tn)]
            out.append(pltpu.make_async_copy(src, stage.at[slot, n], sem.at[slot, n]))
        return out

    def cast_slab(slot, r, dst):
        rows = pl.ds(pl.multiple_of(r * ks, ks), ks)
        for n in range(n_w):
            cols = slice(None) if w_is_nk else slice(n * tn, (n + 1) * tn)
            dst[rows, cols] = stage[slot, n].astype(BF16)

    @pl.when(lin == 0)
    def _():
        for r in range(mi):
            for c in slab_copies(r, r % 2):
                c.start()
            for c in slab_copies(r, r % 2):
                c.wait()
            cast_slab(r % 2, r, wbf[0])
        for c in slab_copies(mi, 0):
            c.start()

    def body(cur, oth):
        slot = lax.rem(i, 2)
        for c in slab_copies(lin + mi, slot):
            c.wait()
        for c in slab_copies(lin + mi + 1, 1 - slot):
            c.start()
        cast_slab(slot, i, oth)
        proj = None
        off = 0
        for a_ref in a_refs:
            kk = a_ref.shape[1]
            if w_is_nk:
                part = lax.dot_general(a_ref[...], cur[:, off:off + kk], _NT,
                                       preferred_element_type=F32)
            else:
                part = jnp.dot(a_ref[...], cur[off:off + kk, :], preferred_element_type=F32)
            proj = part if proj is None else proj + part
            off += kk
        if has_res:
            proj = proj + r_ref[...]
        if shortconv:
            @pl.when(lax.rem(i, blocks_per_seq) == 0)
            def _():
                hist[...] = jnp.zeros_like(hist)

            xin, gate_b, gate_c, z = (proj[:, n * tn:(n + 1) * tn] for n in range(4))
            conv = _causal_conv(gate_c * xin, hist, cw_ref[...], SC_CONV_W, tm, False)
            proj = (gate_b * conv) * (z * _sigmoid(z))
        o_ref[...] = proj.astype(o_ref.dtype)

    @pl.when(lax.rem(j, 2) == 0)
    def _():
        body(wbf[0], wbf[1])

    @pl.when(lax.rem(j, 2) == 1)
    def _():
        body(wbf[1], wbf[0])

    @pl.when(lin == n_steps - 1)
    def _():
        for c in slab_copies(lin + mi + 1, 1 - lax.rem(i, 2)):
            c.wait()


def _stream_matmul(a_parts, w3, layer, n_cols, *, w_is_nk=False, conv_w3=None, seq=None, res=None,
                   out_dtype=F32, tm=1024, tn=1024, name="stream_matmul"):
    m = a_parts[0].shape[0]
    k = sum(a.shape[1] for a in a_parts)
    shortconv = conv_w3 is not None
    n_w = 4 if shortconv else 1
    tm = min(tm, seq if shortconv else m)
    if (m // tm) % 2:
        tm //= 2
    tn = max(t for t in range(LANES, min(tn, n_cols) + 1, LANES) if n_cols % t == 0)
    mi, nj = m // tm, n_cols // tn
    assert m % tm == 0 and mi % 2 == 0 and not (shortconv and w_is_nk)
    if w_is_nk:
        ks = tn // mi
        wbf_shape, stage_shape = (tn, k), (2, 1, ks, k)
        assert tn % mi == 0 and ks % 16 == 0
    else:
        ks = k // mi
        wbf_shape, stage_shape = (k, n_w * tn), (2, n_w, ks, tn)
        assert k % mi == 0 and ks % 16 == 0
    in_specs = [pl.BlockSpec((tm, a.shape[1]), lambda j, i: (i, 0)) for a in a_parts]
    in_specs.append(pl.BlockSpec(memory_space=pl.ANY))
    args = list(a_parts) + [w3]
    scratch = [pltpu.VMEM(wbf_shape, BF16), pltpu.VMEM(wbf_shape, BF16),
               pltpu.VMEM(stage_shape, F32), pltpu.SemaphoreType.DMA((2, n_w))]
    if shortconv:
        in_specs.append(pl.BlockSpec((None, SC_CONV_W, tn), lambda j, i: (layer, 0, j)))
        args.append(conv_w3)
        scratch.append(pltpu.VMEM((HIST, tn), F32))
    if res is not None:
        in_specs.append(pl.BlockSpec((tm, tn), lambda j, i: (i, j)))
        args.append(res)
    return pl.pallas_call(
        functools.partial(_stream_kernel, n_a=len(a_parts), n_w=n_w, w_is_nk=w_is_nk,
                          shortconv=shortconv, has_res=res is not None, mi=mi,
                          nj=nj, nb=nj, tm=tm, tn=tn, ks=ks, layer=layer,
                          blocks_per_seq=(seq // tm) if shortconv else 1),
        grid=(nj, mi),
        in_specs=in_specs,
        out_specs=pl.BlockSpec((tm, tn), lambda j, i: (i, j)),
        out_shape=jax.ShapeDtypeStruct((m, n_cols), out_dtype),
        scratch_shapes=scratch,
        compiler_params=_params(("arbitrary", "arbitrary"),
                                VMEM_LIMIT_RES if res is not None else VMEM_LIMIT),
        name=name,
    )(*args)


def kernel(x, even_norm_w, even_w_in, lru_conv_w, lru_conv_b, lru_w_r, lru_b_r, lru_w_i, lru_b_i,
           lru_lambda, dn_conv_w, dn_a_log, dn_dt_bias, dn_norm_w, even_w_out, odd_norm_w,
           odd_w_in, odd_conv_w, odd_w_out, final_norm_w):
    bsz, seq, d = x.shape
    x2 = x.reshape(bsz * seq, d)
    d_lru = lru_lambda.shape[1]
    d_dn = dn_conv_w.shape[2] // 3
    nheads = d_dn // HEAD_DIM
    n_main = 2 * d_lru + 4 * d_dn
    depth = even_norm_w.shape[0] + odd_norm_w.shape[0]
    for layer in range(depth):
        j = layer // 2
        if layer % 2 == 0:
            w_in_nk = jnp.swapaxes(even_w_in, 1, 2)
            w_tail = jnp.pad(w_in_nk[j, n_main:, :], ((0, LANES - 2 * nheads), (0, 0)))
            h, gates = _norm_gates(x2, even_norm_w[j], w_tail, dn_a_log[j], dn_dt_bias[j], nheads)
            p = _stream_matmul([h], w_in_nk, j, n_main, w_is_nk=True, name="in_proj0")
            ya = _rglru(p, bsz, seq, d_lru, lru_conv_w[j], lru_conv_b[j], lru_w_r[j], lru_b_r[j],
                        lru_w_i[j], lru_b_i[j], lru_lambda[j])
            yb = _deltanet(p, gates, bsz, seq, d_lru, d_dn, dn_conv_w, j, dn_norm_w[j])
            x2 = _stream_matmul([ya, yb], even_w_out, j, d, res=x2, name="out_proj0")
        else:
            h = _rmsnorm(x2, odd_norm_w[j], BF16)
            y = _stream_matmul([h], odd_w_in, j, odd_conv_w.shape[2], conv_w3=odd_conv_w, seq=seq,
                               out_dtype=BF16, tn=SC_TN, name="shortconv")
            x2 = _stream_matmul([y], odd_w_out, j, d, res=x2, name="out_proj1")
    return _rmsnorm(x2, final_norm_w, x.dtype).reshape(bsz, seq, d)
```

```python
import functools

import jax
import jax.numpy as jnp
from jax import lax
from jax.experimental import pallas as pl
from jax.experimental.pallas import tpu as pltpu

HEAD_DIM = 128
CHUNK = 64
GROUP = 2 * CHUNK
LANES = 128
SUBLANES = 8
CONV_W = 4
SC_CONV_W = 3
LRU_C = 8.0
EPS = 1e-6
HIST = 8
VMEM_LIMIT = 56 * 1024 * 1024
VMEM_LIMIT_RES = 61 * 1024 * 1024
SC_TN = 256

F32 = jnp.float32
BF16 = jnp.bfloat16

assert GROUP == HEAD_DIM == LANES


def _params(sem, vmem=VMEM_LIMIT):
    return pltpu.CompilerParams(dimension_semantics=sem, vmem_limit_bytes=vmem)


def _sigmoid(x):
    return jax.nn.sigmoid(x)


def _softplus(x):
    return jnp.maximum(x, 0.0) + jnp.log1p(jnp.exp(-jnp.abs(x)))


def _dot(a, b, dims=(((1,), (0,)), ((), ()))):
    return lax.dot_general(a.astype(BF16), b.astype(BF16), dims, preferred_element_type=F32)


_NT = (((1,), (1,)), ((), ()))


def _rmsnorm_kernel(x_ref, w_ref, o_ref):
    x = x_ref[...]
    ms = jnp.mean(x * x, axis=-1, keepdims=True)
    o_ref[...] = ((x * lax.rsqrt(ms + EPS)) * w_ref[...]).astype(o_ref.dtype)


def _rmsnorm(x2, w, out_dtype, tm=256):
    m, d = x2.shape
    tm = min(tm, m)
    return pl.pallas_call(
        _rmsnorm_kernel,
        grid=(m // tm,),
        in_specs=[pl.BlockSpec((tm, d), lambda i: (i, 0)),
                  pl.BlockSpec((1, d), lambda i: (0, 0))],
        out_specs=pl.BlockSpec((tm, d), lambda i: (i, 0)),
        out_shape=jax.ShapeDtypeStruct((m, d), out_dtype),
        compiler_params=_params(("arbitrary",)),
        name="rmsnorm",
    )(x2, w.reshape(1, d))


def _causal_conv(x_raw, hist, cw, width, tt, first):
    ext = jnp.concatenate([jnp.where(first, 0.0, hist[...]), x_raw], axis=0)
    hist[...] = x_raw[tt - HIST:tt, :]
    z = cw[0:1, :] * ext
    for kk in range(1, width):
        z = cw[kk:kk + 1, :] * ext + pltpu.roll(z, 1, axis=0)
    return z[HIST:HIST + tt, :]


def _rglru_kernel(xa_ref, ga_ref, cw_ref, cb_ref, wr_ref, br_ref, wi_ref, bi_ref, lam_ref,
                  o_ref, xbuf, hcar, *, tt, tc):
    t = pl.program_id(2)

    @pl.when(t == 0)
    def _():
        hcar[...] = jnp.zeros_like(hcar)

    xc = _causal_conv(xa_ref[...], xbuf, cw_ref[...], CONV_W, tt, t == 0) + cb_ref[...]

    xcb = xc.astype(BF16)
    r_parts, i_parts = [], []
    for hh in range(tc // HEAD_DIM):
        xh = xcb[:, hh * HEAD_DIM:(hh + 1) * HEAD_DIM]
        r_parts.append(jnp.dot(xh, wr_ref[hh].astype(BF16), preferred_element_type=F32))
        i_parts.append(jnp.dot(xh, wi_ref[hh].astype(BF16), preferred_element_type=F32))
    r = _sigmoid(jnp.concatenate(r_parts, axis=1) + br_ref[...])
    gi = _sigmoid(jnp.concatenate(i_parts, axis=1) + bi_ref[...])

    log_a = (-LRU_C) * r * _softplus(-lam_ref[...])
    a = jnp.exp(log_a)
    s2 = -jnp.tanh(log_a) * (a * a + 1.0)
    u = jnp.where(s2 == 0.0, 0.0, s2 * lax.rsqrt(s2)) * (gi * xc)

    a = a.reshape(tt // SUBLANES, SUBLANES, tc)
    u = u.reshape(tt // SUBLANES, SUBLANES, tc)
    rin = lax.broadcasted_iota(jnp.int32, a.shape, 1)
    s = 1
    while s < SUBLANES:
        keep = rin >= s
        a_sh = jnp.where(keep, pltpu.roll(a, s, axis=1), 1.0)
        u_sh = jnp.where(keep, pltpu.roll(u, s, axis=1), 0.0)
        u = a * u_sh + u
        a = a * a_sh
        s *= 2
    a = a.reshape(tt, tc)
    u = u.reshape(tt, tc)
    carry = hcar[...]
    tiles = []
    for i in range(tt // SUBLANES):
        rows = slice(i * SUBLANES, (i + 1) * SUBLANES)
        h_i = a[rows] * carry + u[rows]
        carry = h_i[SUBLANES - 1:SUBLANES, :]
        tiles.append(h_i)
    h = jnp.concatenate(tiles, axis=0)
    hcar[...] = carry

    ga = ga_ref[...]
    o_ref[...] = (h * (ga * _sigmoid(ga))).astype(o_ref.dtype)


def _rglru(p, bsz, seq, d_lru, cw, cb, w_r, b_r, w_i, b_i, lam, tt=256, tc=512):
    tt = min(tt, seq)
    tc = min(tc, d_lru)
    nt, nc = seq // tt, d_lru // tc
    hpb = tc // HEAD_DIM
    row = lambda v: v.reshape(1, d_lru)
    vec_spec = pl.BlockSpec((1, tc), lambda b, j, t: (0, j))
    return pl.pallas_call(
        functools.partial(_rglru_kernel, tt=tt, tc=tc),
        grid=(bsz, nc, nt),
        in_specs=[
            pl.BlockSpec((tt, tc), lambda b, j, t: (b * nt + t, j)),
            pl.BlockSpec((tt, tc), lambda b, j, t: (b * nt + t, nc + j)),
            pl.BlockSpec((CONV_W, tc), lambda b, j, t: (0, j)),
            vec_spec,
            pl.BlockSpec((hpb, HEAD_DIM, HEAD_DIM), lambda b, j, t: (j, 0, 0)),
            vec_spec,
            pl.BlockSpec((hpb, HEAD_DIM, HEAD_DIM), lambda b, j, t: (j, 0, 0)),
            vec_spec,
            vec_spec,
        ],
        out_specs=pl.BlockSpec((tt, tc), lambda b, j, t: (b * nt + t, j)),
        out_shape=jax.ShapeDtypeStruct((bsz * seq, d_lru), BF16),
        scratch_shapes=[pltpu.VMEM((HIST, tc), F32), pltpu.VMEM((1, tc), F32)],
        compiler_params=_params(("arbitrary", "arbitrary", "arbitrary")),
        name="rglru",
    )(p, p, cw, row(cb), w_r, row(b_r), w_i, row(b_i), row(lam))


def _norm_gates_kernel(x_ref, w_ref, wt_ref, alog_ref, dtb_ref, h_ref, g_ref, wt_bf,
                       *, nheads, tm):
    @pl.when(pl.program_id(0) == 0)
    def _():
        wt_bf[...] = wt_ref[...].astype(BF16)

    x = x_ref[...]
    ms = jnp.mean(x * x, axis=-1, keepdims=True)
    h = ((x * lax.rsqrt(ms + EPS)) * w_ref[...]).astype(BF16)
    h_ref[...] = h
    tail = lax.dot_general(h, wt_bf[...], _NT, preferred_element_type=F32)

    lane = lax.broadcasted_iota(jnp.int32, (tm, LANES), 1)
    is_a = (lane >= nheads) & (lane < 2 * nheads)
    g = jnp.where(is_a, -jnp.exp(alog_ref[...]) * _softplus(tail + dtb_ref[...]), 0.0)
    rmod = lax.broadcasted_iota(jnp.int32, (tm, LANES), 0) & (CHUNK - 1)
    s = 1
    while s < CHUNK:
        g = g + jnp.where(rmod >= s, pltpu.roll(g, s, axis=0), 0.0)
        s *= 2
    g_ref[...] = jnp.where(is_a, g, _sigmoid(tail))


def _norm_gates(x2, w, w_tail_nk, a_log, dt_bias, nheads, tm=256):
    m, d = x2.shape
    tm = min(tm, m)
    assert tm % CHUNK == 0
    pad_row = lambda v: jnp.pad(v, (nheads, LANES - 2 * nheads)).reshape(1, LANES)
    const = lambda shape: pl.BlockSpec(shape, lambda i: (0, 0))
    return pl.pallas_call(
        functools.partial(_norm_gates_kernel, nheads=nheads, tm=tm),
        grid=(m // tm,),
        in_specs=[pl.BlockSpec((tm, d), lambda i: (i, 0)), const((1, d)), const((LANES, d)),
                  const((1, LANES)), const((1, LANES))],
        out_specs=[pl.BlockSpec((tm, d), lambda i: (i, 0)),
                   pl.BlockSpec((tm, LANES), lambda i: (i, 0))],
        out_shape=[jax.ShapeDtypeStruct((m, d), BF16), jax.ShapeDtypeStruct((m, LANES), F32)],
        scratch_shapes=[pltpu.VMEM((LANES, d), BF16)],
        compiler_params=_params(("arbitrary",)),
        name="norm_gates",
    )(x2, w.reshape(1, d), w_tail_nk, pad_row(a_log), pad_row(dt_bias))


def _dn_prep_kernel(q_ref, k_ref, v_ref, gate_ref, cwq_ref, cwk_ref, cwv_ref,
                    u_ref, wq_ref, kdt_ref, attn_ref, gl_ref,
                    qbuf, kbuf, vbuf, *, tt, hg, nheads):
    hp = pl.program_id(1)
    first = pl.program_id(2) == 0

    @pl.when(first)
    def _():
        for buf in (qbuf, kbuf, vbuf):
            buf[...] = jnp.zeros_like(buf)

    def conv_silu(ref, buf, cw_ref, lo, cs):
        if lo == 0:
            ext = jnp.concatenate([buf[:, cs], ref[0:GROUP, cs]], axis=0)
        else:
            ext = ref[lo - HIST:lo + GROUP, cs]
        z = cw_ref[0:1, cs] * ext
        for kk in range(1, CONV_W):
            z = cw_ref[kk:kk + 1, cs] * ext + pltpu.roll(z, 1, axis=0)
        y = z[HIST:HIST + GROUP, :]
        return y * _sigmoid(y)

    lane = lax.broadcasted_iota(jnp.int32, (GROUP, LANES), 1)
    ii = lax.broadcasted_iota(jnp.int32, (GROUP, GROUP), 0)
    jj = lax.broadcasted_iota(jnp.int32, (GROUP, GROUP), 1)
    same_chunk = (jnp.bitwise_xor(ii, jj) & CHUNK) == 0
    causal = same_chunk & (ii >= jj)
    strict = same_chunk & (ii > jj)
    eye = jnp.where(ii == jj, 1.0, 0.0)
    top = ii < CHUNK

    chains = []
    for hh in range(hg):
        hd = hp * hg + hh
        cs = slice(hh * HEAD_DIM, (hh + 1) * HEAD_DIM)
        for g in range(tt // GROUP):
            lo = g * GROUP
            rs = slice(lo, lo + GROUP)
            q = conv_silu(q_ref, qbuf, cwq_ref, lo, cs)
            k = conv_silu(k_ref, kbuf, cwk_ref, lo, cs)
            v = conv_silu(v_ref, vbuf, cwv_ref, lo, cs)
            q = (q * lax.rsqrt(jnp.sum(q * q, axis=-1, keepdims=True) + EPS)) * (HEAD_DIM ** -0.5)
            k = k * lax.rsqrt(jnp.sum(k * k, axis=-1, keepdims=True) + EPS)
            gate = gate_ref[rs, :]
            beta = jnp.broadcast_to(
                jnp.sum(jnp.where(lane == hd, gate, 0.0), axis=-1, keepdims=True), (GROUP, LANES))
            gc = jnp.broadcast_to(
                jnp.sum(jnp.where(lane == hd + nheads, gate, 0.0), axis=-1, keepdims=True),
                (GROUP, LANES))
            chains.append(dict(cs=cs, g=g, rs=rs, q=q, k=k, v=v, beta=beta, gc=gc,
                               egc=jnp.exp(gc)))
    for ref, buf in ((q_ref, qbuf), (k_ref, kbuf), (v_ref, vbuf)):
        buf[...] = ref[tt - HIST:tt, :]

    for ch in chains:
        gcg = ch["gc"]
        ch["decay"] = jnp.exp(jnp.where(causal, gcg - gcg.T, -jnp.inf))
        ch["kb"] = ch["k"] * ch["beta"]
    for ch in chains:
        ch["kk"] = _dot(ch["kb"], ch["k"], _NT)
    for ch in chains:
        ch["qk"] = _dot(ch["q"], ch["k"], _NT)
    for ch in chains:
        ch["p"] = -jnp.where(strict, ch["kk"] * ch["decay"], 0.0)
        ch["t"] = eye + ch["p"]
    pw = 2
    while pw < CHUNK:
        for ch in chains:
            ch["p"] = _dot(ch["p"], ch["p"])
        for ch in chains:
            ch["t"] = ch["t"] + _dot(ch["p"], ch["t"])
        pw *= 2
    for ch in chains:
        rhs = jnp.concatenate([ch["v"] * ch["beta"], ch["kb"] * ch["egc"]], axis=1)
        ch["sol"] = _dot(ch["t"], rhs)

    for ch in chains:
        cs, g, rs, gcg = ch["cs"], ch["g"], ch["rs"], ch["gc"]
        lo = g * GROUP
        u = ch["sol"][:, 0:HEAD_DIM]
        w = ch["sol"][:, HEAD_DIM:2 * HEAD_DIM].astype(BF16)
        attn = ch["qk"] * ch["decay"]
        q_dec = (ch["q"] * ch["egc"]).astype(BF16)
        gl0 = gcg[CHUNK - 1:CHUNK, :]
        gl1 = gcg[GROUP - 1:GROUP, :]
        k_dec = ch["k"] * jnp.exp(jnp.where(top, gl0, gl1) - gcg)

        u_ref[rs, cs] = u
        for c in range(2):
            cr = slice(c * CHUNK, (c + 1) * CHUNK)
            wq_ref[2 * lo + c * GROUP:2 * lo + c * GROUP + CHUNK, cs] = w[cr]
            wq_ref[2 * lo + c * GROUP + CHUNK:2 * lo + (c + 1) * GROUP, cs] = q_dec[cr]
        kdt_ref[rs, cs] = k_dec.T.astype(BF16)
        attn_ref[rs, cs] = attn.astype(BF16)
        gl_ref[2 * g:2 * g + 1, cs] = jnp.exp(gl0)
        gl_ref[2 * g + 1:2 * g + 2, cs] = jnp.exp(gl1)


def _dn_scan_kernel(u_ref, wq_ref, kdt_ref, attn_ref, gl_ref, z_ref, nw_ref, o_ref, s_ref,
                    *, tt, hg):
    @pl.when(pl.program_id(2) == 0)
    def _():
        s_ref[...] = jnp.zeros_like(s_ref)

    nw = nw_ref[...]
    zeros = jnp.zeros((CHUNK, HEAD_DIM), F32)
    cols = [slice(hh * HEAD_DIM, (hh + 1) * HEAD_DIM) for hh in range(hg)]
    s_mats = [s_ref[hh] for hh in range(hg)]
    for g in range(tt // GROUP):
        lo = g * GROUP
        rs = slice(lo, lo + GROUP)
        v_prev = [zeros] * hg
        outs = [[] for _ in range(hg)]
        for c in range(2):
            r = [_dot(wq_ref[2 * lo + c * GROUP:2 * lo + (c + 1) * GROUP, cs], s_mats[hh])
                 for hh, cs in enumerate(cols)]
            v_new = [u_ref[lo + c * CHUNK:lo + (c + 1) * CHUNK, cs] - r[hh][0:CHUNK]
                     for hh, cs in enumerate(cols)]
            for hh, cs in enumerate(cols):
                v_upd = jnp.concatenate([zeros, v_new[hh]] if c else [v_new[hh], zeros], axis=0)
                s_mats[hh] = (s_mats[hh] * gl_ref[2 * g + c:2 * g + c + 1, cs]
                              + _dot(kdt_ref[rs, cs], v_upd))
            for hh, cs in enumerate(cols):
                v_att = jnp.concatenate([v_prev[hh], v_new[hh]] if c else [v_new[hh], zeros], axis=0)
                outs[hh].append(r[hh][CHUNK:GROUP]
                                + _dot(attn_ref[lo + c * CHUNK:lo + (c + 1) * CHUNK, cs], v_att))
            v_prev = v_new
        for hh, cs in enumerate(cols):
            o = jnp.concatenate(outs[hh], axis=0)
            ms = jnp.mean(o * o, axis=-1, keepdims=True)
            z = z_ref[rs, cs]
            o_ref[rs, cs] = (((o * lax.rsqrt(ms + EPS)) * nw) * (z * _sigmoid(z))).astype(o_ref.dtype)
    for hh in range(hg):
        s_ref[hh] = s_mats[hh]


def _deltanet(p, gates, bsz, seq, d_lru, d_dn, conv_w3, layer, norm_w, tt=512, hg_prep=4,
              hg_scan=8):
    nheads = d_dn // HEAD_DIM
    m = bsz * seq
    tt = min(tt, seq)
    hg_prep = min(hg_prep, nheads)
    hg_scan = min(hg_scan, nheads)
    nt = seq // tt
    base = 2 * d_lru // HEAD_DIM
    ncb = 2 * tt // GROUP

    def prep_cols(off):
        return pl.BlockSpec((tt, hg_prep * HEAD_DIM),
                            lambda b, h, t: (b * nt + t, (base + off * nheads) // hg_prep + h))

    def prep_cw(off):
        return pl.BlockSpec((None, CONV_W, hg_prep * HEAD_DIM),
                            lambda b, h, t: (layer, 0, off * nheads // hg_prep + h))

    assert base % hg_prep == 0 and nheads % hg_prep == 0 and nheads % hg_scan == 0
    wide = hg_prep * HEAD_DIM
    row_blk = lambda rows: pl.BlockSpec((rows, wide), lambda b, h, t: (b * nt + t, h))
    u, wq, kdt, attn, gl = pl.pallas_call(
        functools.partial(_dn_prep_kernel, tt=tt, hg=hg_prep, nheads=nheads),
        grid=(bsz, nheads // hg_prep, nt),
        in_specs=[prep_cols(0), prep_cols(1), prep_cols(2),
                  pl.BlockSpec((tt, LANES), lambda b, h, t: (b * nt + t, 0)),
                  prep_cw(0), prep_cw(1), prep_cw(2)],
        out_specs=[row_blk(tt), row_blk(2 * tt), row_blk(tt), row_blk(tt), row_blk(ncb)],
        out_shape=[jax.ShapeDtypeStruct((m, d_dn), F32),
                   jax.ShapeDtypeStruct((2 * m, d_dn), BF16),
                   jax.ShapeDtypeStruct((m, d_dn), BF16),
                   jax.ShapeDtypeStruct((m, d_dn), BF16),
                   jax.ShapeDtypeStruct((m // CHUNK, d_dn), F32)],
        scratch_shapes=[pltpu.VMEM((HIST, wide), F32)] * 3,
        compiler_params=_params(("arbitrary", "arbitrary", "arbitrary")),
        name="dn_prep",
    )(p, p, p, gates, conv_w3, conv_w3, conv_w3)

    wide = hg_scan * HEAD_DIM
    row_blk = lambda rows: pl.BlockSpec((rows, wide), lambda b, h, t: (b * nt + t, h))
    z_off = (base + 3 * nheads) // hg_scan
    assert (base + 3 * nheads) % hg_scan == 0
    return pl.pallas_call(
        functools.partial(_dn_scan_kernel, tt=tt, hg=hg_scan),
        grid=(bsz, nheads // hg_scan, nt),
        in_specs=[row_blk(tt), row_blk(2 * tt), row_blk(tt), row_blk(tt), row_blk(ncb),
                  pl.BlockSpec((tt, wide), lambda b, h, t: (b * nt + t, z_off + h)),
                  pl.BlockSpec((1, HEAD_DIM), lambda b, h, t: (0, 0))],
        out_specs=row_blk(tt),
        out_shape=jax.ShapeDtypeStruct((m, d_dn), BF16),
        scratch_shapes=[pltpu.VMEM((hg_scan, HEAD_DIM, HEAD_DIM), F32)],
        compiler_params=_params(("arbitrary", "arbitrary", "arbitrary")),
        name="dn_scan",
    )(u, wq, kdt, attn, gl, p, norm_w.reshape(1, HEAD_DIM))


def _stream_kernel(*refs, n_a, n_w, w_is_nk, shortconv, has_res, mi, nj, nb, tm, tn, ks, layer,
                   blocks_per_seq):
    a_refs, w_hbm = refs[:n_a], refs[n_a]
    rest = list(refs[n_a + 1:])
    cw_ref = rest.pop(0) if shortconv else None
    r_ref = rest.pop(0) if has_res else None
    o_ref, wbf0, wbf1, stage, sem = rest[:5]
    hist = rest[5] if shortconv else None
    wbf = (wbf0, wbf1)
    j = pl.program_id(0)
    i = pl.program_id(1)
    lin = j * mi + i
    n_steps = nj * mi

    def slab_copies(l, slot):
        jb = lax.rem(lax.div(l, mi), nj)
        r = lax.rem(l, mi)
        out = []
        for n in range(n_w):
            if w_is_nk:
                src = w_hbm.at[layer, pl.ds(jb * tn + r * ks, ks), :]
            else:
                src = w_hbm.at[layer, pl.ds(r * ks, ks), pl.ds((n * nb + jb) * tn, tn)]
            out.append(pltpu.make_async_copy(src, stage.at[slot, n], sem.at[slot, n]))
        return out

    def cast_slab(slot, r, dst):
        rows = pl.ds(pl.multiple_of(r * ks, ks), ks)
        for n in range(n_w):
            cols = slice(None) if w_is_nk else slice(n * tn, (n + 1) * tn)
            dst[rows, cols] = stage[slot, n].astype(BF16)

    @pl.when(lin == 0)
    def _():
        for c in slab_copies(0, 0):
            c.start()
        for r in range(mi):
            for c in slab_copies(r + 1, (r + 1) % 2):
                c.start()
            for c in slab_copies(r, r % 2):
                c.wait()
            cast_slab(r % 2, r, wbf[0])

    def body(cur, oth):
        slot = lax.rem(i, 2)
        for c in slab_copies(lin + mi, slot):
            c.wait()
        for c in slab_copies(lin + mi + 1, 1 - slot):
            c.start()
        cast_slab(slot, i, oth)
        proj = None
        off = 0
        for a_ref in a_refs:
            kk = a_ref.shape[1]
            if w_is_nk:
                part = lax.dot_general(a_ref[...], cur[:, off:off + kk], _NT,
                                       preferred_element_type=F32)
            else:
                part = jnp.dot(a_ref[...], cur[off:off + kk, :], preferred_element_type=F32)
            proj = part if proj is None else proj + part
            off += kk
        if has_res:
            proj = proj + r_ref[...]
        if shortconv:
            @pl.when(lax.rem(i, blocks_per_seq) == 0)
            def _():
                hist[...] = jnp.zeros_like(hist)

            xin, gate_b, gate_c, z = (proj[:, n * tn:(n + 1) * tn] for n in range(4))
            conv = _causal_conv(gate_c * xin, hist, cw_ref[...], SC_CONV_W, tm, False)
            proj = (gate_b * conv) * (z * _sigmoid(z))
        o_ref[...] = proj.astype(o_ref.dtype)

    @pl.when(lax.rem(j, 2) == 0)
    def _():
        body(wbf[0], wbf[1])

    @pl.when(lax.rem(j, 2) == 1)
    def _():
        body(wbf[1], wbf[0])

    @pl.when(lin == n_steps - 1)
    def _():
        for c in slab_copies(lin + mi + 1, 1 - lax.rem(i, 2)):
            c.wait()


def _stream_matmul(a_parts, w3, layer, n_cols, *, w_is_nk=False, conv_w3=None, seq=None, res=None,
                   out_dtype=F32, tm=1024, tn=1024, name="stream_matmul"):
    m = a_parts[0].shape[0]
    k = sum(a.shape[1] for a in a_parts)
    shortconv = conv_w3 is not None
    n_w = 4 if shortconv else 1
    tm = min(tm, seq if shortconv else m)
    if (m // tm) % 2:
        tm //= 2
    tn = max(t for t in range(LANES, min(tn, n_cols) + 1, LANES) if n_cols % t == 0)
    mi, nj = m // tm, n_cols // tn
    assert m % tm == 0 and mi % 2 == 0 and not (shortconv and w_is_nk)
    if w_is_nk:
        ks = tn // mi
        wbf_shape, stage_shape = (tn, k), (2, 1, ks, k)
        assert tn % mi == 0 and ks % 16 == 0
    else:
        ks = k // mi
        wbf_shape, stage_shape = (k, n_w * tn), (2, n_w, ks, tn)
        assert k % mi == 0 and ks % 16 == 0
    in_specs = [pl.BlockSpec((tm, a.shape[1]), lambda j, i: (i, 0)) for a in a_parts]
    in_specs.append(pl.BlockSpec(memory_space=pl.ANY))
    args = list(a_parts) + [w3]
    scratch = [pltpu.VMEM(wbf_shape, BF16), pltpu.VMEM(wbf_shape, BF16),
               pltpu.VMEM(stage_shape, F32), pltpu.SemaphoreType.DMA((2, n_w))]
    if shortconv:
        in_specs.append(pl.BlockSpec((None, SC_CONV_W, tn), lambda j, i: (layer, 0, j)))
        args.append(conv_w3)
        scratch.append(pltpu.VMEM((HIST, tn), F32))
    if res is not None:
        in_specs.append(pl.BlockSpec((tm, tn), lambda j, i: (i, j)))
        args.append(res)
    return pl.pallas_call(
        functools.partial(_stream_kernel, n_a=len(a_parts), n_w=n_w, w_is_nk=w_is_nk,
                          shortconv=shortconv, has_res=res is not None, mi=mi,
                          nj=nj, nb=nj, tm=tm, tn=tn, ks=ks, layer=layer,
                          blocks_per_seq=(seq // tm) if shortconv else 1),
        grid=(nj, mi),
        in_specs=in_specs,
        out_specs=pl.BlockSpec((tm, tn), lambda j, i: (i, j)),
        out_shape=jax.ShapeDtypeStruct((m, n_cols), out_dtype),
        scratch_shapes=scratch,
        compiler_params=_params(("arbitrary", "arbitrary"),
                                VMEM_LIMIT_RES if res is not None else VMEM_LIMIT),
        name=name,
    )(*args)


def kernel(x, even_norm_w, even_w_in, lru_conv_w, lru_conv_b, lru_w_r, lru_b_r, lru_w_i, lru_b_i,
           lru_lambda, dn_conv_w, dn_a_log, dn_dt_bias, dn_norm_w, even_w_out, odd_norm_w,
           odd_w_in, odd_conv_w, odd_w_out, final_norm_w):
    bsz, seq, d = x.shape
    x2 = x.reshape(bsz * seq, d)
    d_lru = lru_lambda.shape[1]
    d_dn = dn_conv_w.shape[2] // 3
    nheads = d_dn // HEAD_DIM
    n_main = 2 * d_lru + 4 * d_dn
    depth = even_norm_w.shape[0] + odd_norm_w.shape[0]
    for layer in range(depth):
        j = layer // 2
        if layer % 2 == 0:
            w_in_nk = jnp.swapaxes(even_w_in, 1, 2)
            w_tail = jnp.pad(w_in_nk[j, n_main:, :], ((0, LANES - 2 * nheads), (0, 0)))
            h, gates = _norm_gates(x2, even_norm_w[j], w_tail, dn_a_log[j], dn_dt_bias[j], nheads)
            p = _stream_matmul([h], w_in_nk, j, n_main, w_is_nk=True, name="in_proj0")
            ya = _rglru(p, bsz, seq, d_lru, lru_conv_w[j], lru_conv_b[j], lru_w_r[j], lru_b_r[j],
                        lru_w_i[j], lru_b_i[j], lru_lambda[j])
            yb = _deltanet(p, gates, bsz, seq, d_lru, d_dn, dn_conv_w, j, dn_norm_w[j])
            x2 = _stream_matmul([ya, yb], even_w_out, j, d, res=x2, name="out_proj0")
        else:
            h = _rmsnorm(x2, odd_norm_w[j], BF16)
            y = _stream_matmul([h], odd_w_in, j, odd_conv_w.shape[2], conv_w3=odd_conv_w, seq=seq,
                               out_dtype=BF16, tn=SC_TN, name="shortconv")
            x2 = _stream_matmul([y], odd_w_out, j, d, res=x2, name="out_proj1")
    return _rmsnorm(x2, final_norm_w, x.dtype).reshape(bsz, seq, d)
```

```python
import functools

import jax
import jax.numpy as jnp
from jax import lax
from jax.experimental import pallas as pl
from jax.experimental.pallas import tpu as pltpu

HEAD_DIM = 128
CHUNK = 64
GROUP = 2 * CHUNK
LANES = 128
SUBLANES = 8
CONV_W = 4
SC_CONV_W = 3
LRU_C = 8.0
EPS = 1e-6
HIST = 8
VMEM_LIMIT = 56 * 1024 * 1024
VMEM_LIMIT_RES = 61 * 1024 * 1024
SC_TN = 256

F32 = jnp.float32
BF16 = jnp.bfloat16

assert GROUP == HEAD_DIM == LANES


def _params(sem, vmem=VMEM_LIMIT):
    return pltpu.CompilerParams(dimension_semantics=sem, vmem_limit_bytes=vmem)


def _sigmoid(x):
    return jax.nn.sigmoid(x)


def _softplus(x):
    return jnp.maximum(x, 0.0) + jnp.log1p(jnp.exp(-jnp.abs(x)))


def _dot(a, b, dims=(((1,), (0,)), ((), ()))):
    return lax.dot_general(a.astype(BF16), b.astype(BF16), dims, preferred_element_type=F32)


_NT = (((1,), (1,)), ((), ()))


def _rmsnorm_kernel(x_ref, w_ref, o_ref):
    x = x_ref[...]
    ms = jnp.mean(x * x, axis=-1, keepdims=True)
    o_ref[...] = ((x * lax.rsqrt(ms + EPS)) * w_ref[...]).astype(o_ref.dtype)


def _rmsnorm(x2, w, out_dtype, tm=512):
    m, d = x2.shape
    tm = min(tm, m)
    return pl.pallas_call(
        _rmsnorm_kernel,
        grid=(m // tm,),
        in_specs=[pl.BlockSpec((tm, d), lambda i: (i, 0)),
                  pl.BlockSpec((1, d), lambda i: (0, 0))],
        out_specs=pl.BlockSpec((tm, d), lambda i: (i, 0)),
        out_shape=jax.ShapeDtypeStruct((m, d), out_dtype),
        compiler_params=_params(("arbitrary",)),
        name="rmsnorm",
    )(x2, w.reshape(1, d))


def _causal_conv(x_raw, hist, cw, width, tt, first):
    ext = jnp.concatenate([jnp.where(first, 0.0, hist[...]), x_raw], axis=0)
    hist[...] = x_raw[tt - HIST:tt, :]
    z = cw[0:1, :] * ext
    for kk in range(1, width):
        z = cw[kk:kk + 1, :] * ext + pltpu.roll(z, 1, axis=0)
    return z[HIST:HIST + tt, :]


def _rglru_kernel(xa_ref, ga_ref, cw_ref, cb_ref, wr_ref, br_ref, wi_ref, bi_ref, lam_ref,
                  o_ref, xbuf, hcar, *, tt, tc):
    t = pl.program_id(2)

    @pl.when(t == 0)
    def _():
        hcar[...] = jnp.zeros_like(hcar)

    xc = _causal_conv(xa_ref[...], xbuf, cw_ref[...], CONV_W, tt, t == 0) + cb_ref[...]

    xcb = xc.astype(BF16)
    r_parts, i_parts = [], []
    for hh in range(tc // HEAD_DIM):
        xh = xcb[:, hh * HEAD_DIM:(hh + 1) * HEAD_DIM]
        r_parts.append(jnp.dot(xh, wr_ref[hh].astype(BF16), preferred_element_type=F32))
        i_parts.append(jnp.dot(xh, wi_ref[hh].astype(BF16), preferred_element_type=F32))
    r = _sigmoid(jnp.concatenate(r_parts, axis=1) + br_ref[...])
    gi = _sigmoid(jnp.concatenate(i_parts, axis=1) + bi_ref[...])

    log_a = (-LRU_C) * r * _softplus(-lam_ref[...])
    a = jnp.exp(log_a)
    s2 = -jnp.tanh(log_a) * (a * a + 1.0)
    u = jnp.where(s2 == 0.0, 0.0, s2 * lax.rsqrt(s2)) * (gi * xc)

    a = a.reshape(tt // SUBLANES, SUBLANES, tc)
    u = u.reshape(tt // SUBLANES, SUBLANES, tc)
    rin = lax.broadcasted_iota(jnp.int32, a.shape, 1)
    s = 1
    while s < SUBLANES:
        keep = rin >= s
        a_sh = jnp.where(keep, pltpu.roll(a, s, axis=1), 1.0)
        u_sh = jnp.where(keep, pltpu.roll(u, s, axis=1), 0.0)
        u = a * u_sh + u
        a = a * a_sh
        s *= 2
    a = a.reshape(tt, tc)
    u = u.reshape(tt, tc)
    carry = hcar[...]
    tiles = []
    for i in range(tt // SUBLANES):
        rows = slice(i * SUBLANES, (i + 1) * SUBLANES)
        h_i = a[rows] * carry + u[rows]
        carry = h_i[SUBLANES - 1:SUBLANES, :]
        tiles.append(h_i)
    h = jnp.concatenate(tiles, axis=0)
    hcar[...] = carry

    ga = ga_ref[...]
    o_ref[...] = (h * (ga * _sigmoid(ga))).astype(o_ref.dtype)


def _rglru(p, bsz, seq, d_lru, cw, cb, w_r, b_r, w_i, b_i, lam, tt=512, tc=1024):
    tt = min(tt, seq)
    tc = min(tc, d_lru)
    nt, nc = seq // tt, d_lru // tc
    hpb = tc // HEAD_DIM
    row = lambda v: v.reshape(1, d_lru)
    vec_spec = pl.BlockSpec((1, tc), lambda b, j, t: (0, j))
    return pl.pallas_call(
        functools.partial(_rglru_kernel, tt=tt, tc=tc),
        grid=(bsz, nc, nt),
        in_specs=[
            pl.BlockSpec((tt, tc), lambda b, j, t: (b * nt + t, j)),
            pl.BlockSpec((tt, tc), lambda b, j, t: (b * nt + t, nc + j)),
            pl.BlockSpec((CONV_W, tc), lambda b, j, t: (0, j)),
            vec_spec,
            pl.BlockSpec((hpb, HEAD_DIM, HEAD_DIM), lambda b, j, t: (j, 0, 0)),
            vec_spec,
            pl.BlockSpec((hpb, HEAD_DIM, HEAD_DIM), lambda b, j, t: (j, 0, 0)),
            vec_spec,
            vec_spec,
        ],
        out_specs=pl.BlockSpec((tt, tc), lambda b, j, t: (b * nt + t, j)),
        out_shape=jax.ShapeDtypeStruct((bsz * seq, d_lru), BF16),
        scratch_shapes=[pltpu.VMEM((HIST, tc), F32), pltpu.VMEM((1, tc), F32)],
        compiler_params=_params(("arbitrary", "arbitrary", "arbitrary")),
        name="rglru",
    )(p, p, cw, row(cb), w_r, row(b_r), w_i, row(b_i), row(lam))


def _norm_gates_kernel(x_ref, w_ref, wt_ref, alog_ref, dtb_ref, h_ref, g_ref, wt_bf,
                       *, nheads, tm):
    @pl.when(pl.program_id(0) == 0)
    def _():
        wt_bf[...] = wt_ref[...].astype(BF16)

    x = x_ref[...]
    ms = jnp.mean(x * x, axis=-1, keepdims=True)
    h = ((x * lax.rsqrt(ms + EPS)) * w_ref[...]).astype(BF16)
    h_ref[...] = h
    tail = lax.dot_general(h, wt_bf[...], _NT, preferred_element_type=F32)

    lane = lax.broadcasted_iota(jnp.int32, (tm, LANES), 1)
    is_a = (lane >= nheads) & (lane < 2 * nheads)
    g = jnp.where(is_a, -jnp.exp(alog_ref[...]) * _softplus(tail + dtb_ref[...]), 0.0)
    rmod = lax.broadcasted_iota(jnp.int32, (tm, LANES), 0) & (CHUNK - 1)
    s = 1
    while s < CHUNK:
        g = g + jnp.where(rmod >= s, pltpu.roll(g, s, axis=0), 0.0)
        s *= 2
    g_ref[...] = jnp.where(is_a, g, _sigmoid(tail))


def _norm_gates(x2, w, w_tail_nk, a_log, dt_bias, nheads, tm=512):
    m, d = x2.shape
    tm = min(tm, m)
    assert tm % CHUNK == 0
    pad_row = lambda v: jnp.pad(v, (nheads, LANES - 2 * nheads)).reshape(1, LANES)
    const = lambda shape: pl.BlockSpec(shape, lambda i: (0, 0))
    return pl.pallas_call(
        functools.partial(_norm_gates_kernel, nheads=nheads, tm=tm),
        grid=(m // tm,),
        in_specs=[pl.BlockSpec((tm, d), lambda i: (i, 0)), const((1, d)), const((LANES, d)),
                  const((1, LANES)), const((1, LANES))],
        out_specs=[pl.BlockSpec((tm, d), lambda i: (i, 0)),
                   pl.BlockSpec((tm, LANES), lambda i: (i, 0))],
        out_shape=[jax.ShapeDtypeStruct((m, d), BF16), jax.ShapeDtypeStruct((m, LANES), F32)],
        scratch_shapes=[pltpu.VMEM((LANES, d), BF16)],
        compiler_params=_params(("arbitrary",)),
        name="norm_gates",
    )(x2, w.reshape(1, d), w_tail_nk, pad_row(a_log), pad_row(dt_bias))


def _dn_prep_kernel(q_ref, k_ref, v_ref, gate_ref, cwq_ref, cwk_ref, cwv_ref,
                    u_ref, wq_ref, kdt_ref, attn_ref, gl_ref,
                    qbuf, kbuf, vbuf, *, tt, hg, nheads):
    hp = pl.program_id(1)
    first = pl.program_id(2) == 0

    @pl.when(first)
    def _():
        for buf in (qbuf, kbuf, vbuf):
            buf[...] = jnp.zeros_like(buf)

    def conv_silu(ref, buf, cw_ref, lo, cs):
        if lo == 0:
            ext = jnp.concatenate([buf[:, cs], ref[0:GROUP, cs]], axis=0)
        else:
            ext = ref[lo - HIST:lo + GROUP, cs]
        z = cw_ref[0:1, cs] * ext
        for kk in range(1, CONV_W):
            z = cw_ref[kk:kk + 1, cs] * ext + pltpu.roll(z, 1, axis=0)
        y = z[HIST:HIST + GROUP, :]
        return y * _sigmoid(y)

    lane = lax.broadcasted_iota(jnp.int32, (GROUP, LANES), 1)
    ii = lax.broadcasted_iota(jnp.int32, (GROUP, GROUP), 0)
    jj = lax.broadcasted_iota(jnp.int32, (GROUP, GROUP), 1)
    same_chunk = (jnp.bitwise_xor(ii, jj) & CHUNK) == 0
    causal = same_chunk & (ii >= jj)
    strict = same_chunk & (ii > jj)
    eye = jnp.where(ii == jj, 1.0, 0.0)
    top = ii < CHUNK

    chains = []
    for hh in range(hg):
        hd = hp * hg + hh
        cs = slice(hh * HEAD_DIM, (hh + 1) * HEAD_DIM)
        for g in range(tt // GROUP):
            lo = g * GROUP
            rs = slice(lo, lo + GROUP)
            q = conv_silu(q_ref, qbuf, cwq_ref, lo, cs)
            k = conv_silu(k_ref, kbuf, cwk_ref, lo, cs)
            v = conv_silu(v_ref, vbuf, cwv_ref, lo, cs)
            q = (q * lax.rsqrt(jnp.sum(q * q, axis=-1, keepdims=True) + EPS)) * (HEAD_DIM ** -0.5)
            k = k * lax.rsqrt(jnp.sum(k * k, axis=-1, keepdims=True) + EPS)
            gate = gate_ref[rs, :]
            beta = jnp.broadcast_to(
                jnp.sum(jnp.where(lane == hd, gate, 0.0), axis=-1, keepdims=True), (GROUP, LANES))
            gc = jnp.broadcast_to(
                jnp.sum(jnp.where(lane == hd + nheads, gate, 0.0), axis=-1, keepdims=True),
                (GROUP, LANES))
            chains.append(dict(cs=cs, g=g, rs=rs, q=q, k=k, v=v, beta=beta, gc=gc,
                               egc=jnp.exp(gc)))
    for ref, buf in ((q_ref, qbuf), (k_ref, kbuf), (v_ref, vbuf)):
        buf[...] = ref[tt - HIST:tt, :]

    for ch in chains:
        gcg = ch["gc"]
        ch["decay"] = jnp.exp(jnp.where(causal, gcg - gcg.T, -jnp.inf))
        ch["kb"] = ch["k"] * ch["beta"]
    for ch in chains:
        ch["kk"] = _dot(ch["kb"], ch["k"], _NT)
    for ch in chains:
        ch["qk"] = _dot(ch["q"], ch["k"], _NT)
    for ch in chains:
        ch["p"] = -jnp.where(strict, ch["kk"] * ch["decay"], 0.0)
        ch["t"] = eye + ch["p"]
    pw = 2
    while pw < CHUNK:
        for ch in chains:
            ch["p"] = _dot(ch["p"], ch["p"])
        for ch in chains:
            ch["t"] = ch["t"] + _dot(ch["p"], ch["t"])
        pw *= 2
    for ch in chains:
        rhs = jnp.concatenate([ch["v"] * ch["beta"], ch["kb"] * ch["egc"]], axis=1)
        ch["sol"] = _dot(ch["t"], rhs)

    for ch in chains:
        cs, g, rs, gcg = ch["cs"], ch["g"], ch["rs"], ch["gc"]
        lo = g * GROUP
        u = ch["sol"][:, 0:HEAD_DIM]
        w = ch["sol"][:, HEAD_DIM:2 * HEAD_DIM].astype(BF16)
        attn = ch["qk"] * ch["decay"]
        q_dec = (ch["q"] * ch["egc"]).astype(BF16)
        gl0 = gcg[CHUNK - 1:CHUNK, :]
        gl1 = gcg[GROUP - 1:GROUP, :]
        k_dec = ch["k"] * jnp.exp(jnp.where(top, gl0, gl1) - gcg)

        u_ref[rs, cs] = u
        for c in range(2):
            cr = slice(c * CHUNK, (c + 1) * CHUNK)
            wq_ref[2 * lo + c * GROUP:2 * lo + c * GROUP + CHUNK, cs] = w[cr]
            wq_ref[2 * lo + c * GROUP + CHUNK:2 * lo + (c + 1) * GROUP, cs] = q_dec[cr]
        kdt_ref[rs, cs] = k_dec.T.astype(BF16)
        attn_ref[rs, cs] = attn.astype(BF16)
        gl_ref[2 * g:2 * g + 1, cs] = jnp.exp(gl0)
        gl_ref[2 * g + 1:2 * g + 2, cs] = jnp.exp(gl1)


def _dn_scan_kernel(u_ref, wq_ref, kdt_ref, attn_ref, gl_ref, z_ref, nw_ref, o_ref, s_ref,
                    *, tt, hg):
    @pl.when(pl.program_id(2) == 0)
    def _():
        s_ref[...] = jnp.zeros_like(s_ref)

    nw = nw_ref[...]
    zeros = jnp.zeros((CHUNK, HEAD_DIM), F32)
    cols = [slice(hh * HEAD_DIM, (hh + 1) * HEAD_DIM) for hh in range(hg)]
    s_mats = [s_ref[hh] for hh in range(hg)]
    for g in range(tt // GROUP):
        lo = g * GROUP
        rs = slice(lo, lo + GROUP)
        v_prev = [zeros] * hg
        outs = [[] for _ in range(hg)]
        for c in range(2):
            r = [_dot(wq_ref[2 * lo + c * GROUP:2 * lo + (c + 1) * GROUP, cs], s_mats[hh])
                 for hh, cs in enumerate(cols)]
            v_new = [u_ref[lo + c * CHUNK:lo + (c + 1) * CHUNK, cs] - r[hh][0:CHUNK]
                     for hh, cs in enumerate(cols)]
            for hh, cs in enumerate(cols):
                v_upd = jnp.concatenate([zeros, v_new[hh]] if c else [v_new[hh], zeros], axis=0)
                s_mats[hh] = (s_mats[hh] * gl_ref[2 * g + c:2 * g + c + 1, cs]
                              + _dot(kdt_ref[rs, cs], v_upd))
            for hh, cs in enumerate(cols):
                v_att = jnp.concatenate([v_prev[hh], v_new[hh]] if c else [v_new[hh], zeros], axis=0)
                outs[hh].append(r[hh][CHUNK:GROUP]
                                + _dot(attn_ref[lo + c * CHUNK:lo + (c + 1) * CHUNK, cs], v_att))
            v_prev = v_new
        for hh, cs in enumerate(cols):
            o = jnp.concatenate(outs[hh], axis=0)
            ms = jnp.mean(o * o, axis=-1, keepdims=True)
            z = z_ref[rs, cs]
            o_ref[rs, cs] = (((o * lax.rsqrt(ms + EPS)) * nw) * (z * _sigmoid(z))).astype(o_ref.dtype)
    for hh in range(hg):
        s_ref[hh] = s_mats[hh]


def _deltanet(p, gates, bsz, seq, d_lru, d_dn, conv_w3, layer, norm_w, tt=512, hg_prep=4,
              hg_scan=16):
    nheads = d_dn // HEAD_DIM
    m = bsz * seq
    tt = min(tt, seq)
    hg_prep = min(hg_prep, nheads)
    hg_scan = min(hg_scan, nheads)
    nt = seq // tt
    base = 2 * d_lru // HEAD_DIM
    ncb = 2 * tt // GROUP

    def prep_cols(off):
        return pl.BlockSpec((tt, hg_prep * HEAD_DIM),
                            lambda b, h, t: (b * nt + t, (base + off * nheads) // hg_prep + h))

    def prep_cw(off):
        return pl.BlockSpec((None, CONV_W, hg_prep * HEAD_DIM),
                            lambda b, h, t: (layer, 0, off * nheads // hg_prep + h))

    assert base % hg_prep == 0 and nheads % hg_prep == 0 and nheads % hg_scan == 0
    wide = hg_prep * HEAD_DIM
    row_blk = lambda rows: pl.BlockSpec((rows, wide), lambda b, h, t: (b * nt + t, h))
    u, wq, kdt, attn, gl = pl.pallas_call(
        functools.partial(_dn_prep_kernel, tt=tt, hg=hg_prep, nheads=nheads),
        grid=(bsz, nheads // hg_prep, nt),
        in_specs=[prep_cols(0), prep_cols(1), prep_cols(2),
                  pl.BlockSpec((tt, LANES), lambda b, h, t: (b * nt + t, 0)),
                  prep_cw(0), prep_cw(1), prep_cw(2)],
        out_specs=[row_blk(tt), row_blk(2 * tt), row_blk(tt), row_blk(tt), row_blk(ncb)],
        out_shape=[jax.ShapeDtypeStruct((m, d_dn), F32),
                   jax.ShapeDtypeStruct((2 * m, d_dn), BF16),
                   jax.ShapeDtypeStruct((m, d_dn), BF16),
                   jax.ShapeDtypeStruct((m, d_dn), BF16),
                   jax.ShapeDtypeStruct((m // CHUNK, d_dn), F32)],
        scratch_shapes=[pltpu.VMEM((HIST, wide), F32)] * 3,
        compiler_params=_params(("arbitrary", "arbitrary", "arbitrary")),
        name="dn_prep",
    )(p, p, p, gates, conv_w3, conv_w3, conv_w3)

    wide = hg_scan * HEAD_DIM
    row_blk = lambda rows: pl.BlockSpec((rows, wide), lambda b, h, t: (b * nt + t, h))
    z_off = (base + 3 * nheads) // hg_scan
    assert (base + 3 * nheads) % hg_scan == 0
    return pl.pallas_call(
        functools.partial(_dn_scan_kernel, tt=tt, hg=hg_scan),
        grid=(bsz, nheads // hg_scan, nt),
        in_specs=[row_blk(tt), row_blk(2 * tt), row_blk(tt), row_blk(tt), row_blk(ncb),
                  pl.BlockSpec((tt, wide), lambda b, h, t: (b * nt + t, z_off + h)),
                  pl.BlockSpec((1, HEAD_DIM), lambda b, h, t: (0, 0))],
        out_specs=row_blk(tt),
        out_shape=jax.ShapeDtypeStruct((m, d_dn), BF16),
        scratch_shapes=[pltpu.VMEM((hg_scan, HEAD_DIM, HEAD_DIM), F32)],
        compiler_params=_params(("arbitrary", "arbitrary", "arbitrary")),
        name="dn_scan",
    )(u, wq, kdt, attn, gl, p, norm_w.reshape(1, HEAD_DIM))


def _stream_kernel(*refs, n_a, n_w, w_is_nk, shortconv, has_res, mi, nj, nb, tm, tn, ks, layer,
                   blocks_per_seq):
    a_refs, w_hbm = refs[:n_a], refs[n_a]
    rest = list(refs[n_a + 1:])
    cw_ref = rest.pop(0) if shortconv else None
    r_ref = rest.pop(0) if has_res else None
    o_ref, wbf0, wbf1, stage, sem = rest[:5]
    hist = rest[5] if shortconv else None
    wbf = (wbf0, wbf1)
    j = pl.program_id(0)
    i = pl.program_id(1)
    lin = j * mi + i
    n_steps = nj * mi

    def slab_copies(l, slot):
        jb = lax.rem(lax.div(l, mi), nj)
        r = lax.rem(l, mi)
        out = []
        for n in range(n_w):
            if w_is_nk:
                src = w_hbm.at[layer, pl.ds(jb * tn + r * ks, ks), :]
            else:
                src = w_hbm.at[layer, pl.ds(r * ks, ks), pl.ds((n * nb + jb) * tn, tn)]
            out.append(pltpu.make_async_copy(src, stage.at[slot, n], sem.at[slot, n]))
        return out

    def cast_slab(slot, r, dst):
        rows = pl.ds(pl.multiple_of(r * ks, ks), ks)
        for n in range(n_w):
            cols = slice(None) if w_is_nk else slice(n * tn, (n + 1) * tn)
            dst[rows, cols] = stage[slot, n].astype(BF16)

    @pl.when(lin == 0)
    def _():
        for c in slab_copies(0, 0):
            c.start()
        for r in range(mi):
            for c in slab_copies(r + 1, (r + 1) % 2):
                c.start()
            for c in slab_copies(r, r % 2):
                c.wait()
            cast_slab(r % 2, r, wbf[0])

    def body(cur, oth):
        slot = lax.rem(i, 2)
        for c in slab_copies(lin + mi, slot):
            c.wait()
        for c in slab_copies(lin + mi + 1, 1 - slot):
            c.start()
        cast_slab(slot, i, oth)
        proj = None
        off = 0
        for a_ref in a_refs:
            kk = a_ref.shape[1]
            if w_is_nk:
                part = lax.dot_general(a_ref[...], cur[:, off:off + kk], _NT,
                                       preferred_element_type=F32)
            else:
                part = jnp.dot(a_ref[...], cur[off:off + kk, :], preferred_element_type=F32)
            proj = part if proj is None else proj + part
            off += kk
        if has_res:
            proj = proj + r_ref[...]
        if shortconv:
            @pl.when(lax.rem(i, blocks_per_seq) == 0)
            def _():
                hist[...] = jnp.zeros_like(hist)

            xin, gate_b, gate_c, z = (proj[:, n * tn:(n + 1) * tn] for n in range(4))
            conv = _causal_conv(gate_c * xin, hist, cw_ref[...], SC_CONV_W, tm, False)
            proj = (gate_b * conv) * (z * _sigmoid(z))
        o_ref[...] = proj.astype(o_ref.dtype)

    @pl.when(lax.rem(j, 2) == 0)
    def _():
        body(wbf[0], wbf[1])

    @pl.when(lax.rem(j, 2) == 1)
    def _():
        body(wbf[1], wbf[0])

    @pl.when(lin == n_steps - 1)
    def _():
        for c in slab_copies(lin + mi + 1, 1 - lax.rem(i, 2)):
            c.wait()


def _stream_matmul(a_parts, w3, layer, n_cols, *, w_is_nk=False, conv_w3=None, seq=None, res=None,
                   out_dtype=F32, tm=1024, tn=1024, name="stream_matmul"):
    m = a_parts[0].shape[0]
    k = sum(a.shape[1] for a in a_parts)
    shortconv = conv_w3 is not None
    n_w = 4 if shortconv else 1
    tm = min(tm, seq if shortconv else m)
    if (m // tm) % 2:
        tm //= 2
    tn = max(t for t in range(LANES, min(tn, n_cols) + 1, LANES) if n_cols % t == 0)
    mi, nj = m // tm, n_cols // tn
    assert m % tm == 0 and mi % 2 == 0 and not (shortconv and w_is_nk)
    if w_is_nk:
        ks = tn // mi
        wbf_shape, stage_shape = (tn, k), (2, 1, ks, k)
        assert tn % mi == 0 and ks % 16 == 0
    else:
        ks = k // mi
        wbf_shape, stage_shape = (k, n_w * tn), (2, n_w, ks, tn)
        assert k % mi == 0 and ks % 16 == 0
    in_specs = [pl.BlockSpec((tm, a.shape[1]), lambda j, i: (i, 0)) for a in a_parts]
    in_specs.append(pl.BlockSpec(memory_space=pl.ANY))
    args = list(a_parts) + [w3]
    scratch = [pltpu.VMEM(wbf_shape, BF16), pltpu.VMEM(wbf_shape, BF16),
               pltpu.VMEM(stage_shape, F32), pltpu.SemaphoreType.DMA((2, n_w))]
    if shortconv:
        in_specs.append(pl.BlockSpec((None, SC_CONV_W, tn), lambda j, i: (layer, 0, j)))
        args.append(conv_w3)
        scratch.append(pltpu.VMEM((HIST, tn), F32))
    if res is not None:
        in_specs.append(pl.BlockSpec((tm, tn), lambda j, i: (i, j)))
        args.append(res)
    return pl.pallas_call(
        functools.partial(_stream_kernel, n_a=len(a_parts), n_w=n_w, w_is_nk=w_is_nk,
                          shortconv=shortconv, has_res=res is not None, mi=mi,
                          nj=nj, nb=nj, tm=tm, tn=tn, ks=ks, layer=layer,
                          blocks_per_seq=(seq // tm) if shortconv else 1),
        grid=(nj, mi),
        in_specs=in_specs,
        out_specs=pl.BlockSpec((tm, tn), lambda j, i: (i, j)),
        out_shape=jax.ShapeDtypeStruct((m, n_cols), out_dtype),
        scratch_shapes=scratch,
        compiler_params=_params(("arbitrary", "arbitrary"),
                                VMEM_LIMIT_RES if res is not None else VMEM_LIMIT),
        name=name,
    )(*args)


def kernel(x, even_norm_w, even_w_in, lru_conv_w, lru_conv_b, lru_w_r, lru_b_r, lru_w_i, lru_b_i,
           lru_lambda, dn_conv_w, dn_a_log, dn_dt_bias, dn_norm_w, even_w_out, odd_norm_w,
           odd_w_in, odd_conv_w, odd_w_out, final_norm_w):
    bsz, seq, d = x.shape
    x2 = x.reshape(bsz * seq, d)
    d_lru = lru_lambda.shape[1]
    d_dn = dn_conv_w.shape[2] // 3
    nheads = d_dn // HEAD_DIM
    n_main = 2 * d_lru + 4 * d_dn
    depth = even_norm_w.shape[0] + odd_norm_w.shape[0]
    for layer in range(depth):
        j = layer // 2
        if layer % 2 == 0:
            w_in_nk = jnp.swapaxes(even_w_in, 1, 2)
            w_tail = jnp.pad(w_in_nk[j, n_main:, :], ((0, LANES - 2 * nheads), (0, 0)))
            h, gates = _norm_gates(x2, even_norm_w[j], w_tail, dn_a_log[j], dn_dt_bias[j], nheads)
            p = _stream_matmul([h], w_in_nk, j, n_main, w_is_nk=True, name="in_proj0")
            ya = _rglru(p, bsz, seq, d_lru, lru_conv_w[j], lru_conv_b[j], lru_w_r[j], lru_b_r[j],
                        lru_w_i[j], lru_b_i[j], lru_lambda[j])
            yb = _deltanet(p, gates, bsz, seq, d_lru, d_dn, dn_conv_w, j, dn_norm_w[j])
            x2 = _stream_matmul([ya, yb], even_w_out, j, d, res=x2, name="out_proj0")
        else:
            h = _rmsnorm(x2, odd_norm_w[j], BF16)
            y = _stream_matmul([h], odd_w_in, j, odd_conv_w.shape[2], conv_w3=odd_conv_w, seq=seq,
                               out_dtype=BF16, tn=SC_TN, name="shortconv")
            x2 = _stream_matmul([y], odd_w_out, j, d, res=x2, name="out_proj1")
    return _rmsnorm(x2, final_norm_w, x.dtype).reshape(bsz, seq, d)
```

```python
import functools

import jax
import jax.numpy as jnp
from jax import lax
from jax.experimental import pallas as pl
from jax.experimental.pallas import tpu as pltpu

HEAD_DIM = 128
CHUNK = 64
GROUP = 2 * CHUNK
LANES = 128
SUBLANES = 8
CONV_W = 4
SC_CONV_W = 3
LRU_C = 8.0
EPS = 1e-6
HIST = 8
VMEM_LIMIT = 56 * 1024 * 1024
VMEM_LIMIT_RES = 61 * 1024 * 1024
SC_TN = 256

F32 = jnp.float32
BF16 = jnp.bfloat16

assert GROUP == HEAD_DIM == LANES


def _params(sem, vmem=VMEM_LIMIT):
    return pltpu.CompilerParams(dimension_semantics=sem, vmem_limit_bytes=vmem)


def _sigmoid(x):
    return jax.nn.sigmoid(x)


def _softplus(x):
    return jnp.maximum(x, 0.0) + jnp.log1p(jnp.exp(-jnp.abs(x)))


def _dot(a, b, dims=(((1,), (0,)), ((), ()))):
    return lax.dot_general(a.astype(BF16), b.astype(BF16), dims, preferred_element_type=F32)


_NT = (((1,), (1,)), ((), ()))


def _rmsnorm_kernel(x_ref, w_ref, o_ref):
    x = x_ref[...]
    ms = jnp.mean(x * x, axis=-1, keepdims=True)
    o_ref[...] = ((x * lax.rsqrt(ms + EPS)) * w_ref[...]).astype(o_ref.dtype)


def _rmsnorm(x2, w, out_dtype, tm=512):
    m, d = x2.shape
    tm = min(tm, m)
    return pl.pallas_call(
        _rmsnorm_kernel,
        grid=(m // tm,),
        in_specs=[pl.BlockSpec((tm, d), lambda i: (i, 0)),
                  pl.BlockSpec((1, d), lambda i: (0, 0))],
        out_specs=pl.BlockSpec((tm, d), lambda i: (i, 0)),
        out_shape=jax.ShapeDtypeStruct((m, d), out_dtype),
        compiler_params=_params(("arbitrary",)),
        name="rmsnorm",
    )(x2, w.reshape(1, d))


def _causal_conv(x_raw, hist, cw, width, tt, first):
    ext = jnp.concatenate([jnp.where(first, 0.0, hist[...]), x_raw], axis=0)
    hist[...] = x_raw[tt - HIST:tt, :]
    z = cw[0:1, :] * ext
    for kk in range(1, width):
        z = cw[kk:kk + 1, :] * ext + pltpu.roll(z, 1, axis=0)
    return z[HIST:HIST + tt, :]


def _rglru_kernel(xa_ref, ga_ref, cw_ref, cb_ref, wr_ref, br_ref, wi_ref, bi_ref, lam_ref,
                  o_ref, xbuf, hcar, *, tt, tc):
    t = pl.program_id(2)

    @pl.when(t == 0)
    def _():
        hcar[...] = jnp.zeros_like(hcar)

    xc = _causal_conv(xa_ref[...], xbuf, cw_ref[...], CONV_W, tt, t == 0) + cb_ref[...]

    xcb = xc.astype(BF16)
    r_parts, i_parts = [], []
    for hh in range(tc // HEAD_DIM):
        xh = xcb[:, hh * HEAD_DIM:(hh + 1) * HEAD_DIM]
        r_parts.append(jnp.dot(xh, wr_ref[hh].astype(BF16), preferred_element_type=F32))
        i_parts.append(jnp.dot(xh, wi_ref[hh].astype(BF16), preferred_element_type=F32))
    r = _sigmoid(jnp.concatenate(r_parts, axis=1) + br_ref[...])
    gi = _sigmoid(jnp.concatenate(i_parts, axis=1) + bi_ref[...])

    log_a = (-LRU_C) * r * _softplus(-lam_ref[...])
    a = jnp.exp(log_a)
    s2 = -jnp.tanh(log_a) * (a * a + 1.0)
    u = jnp.where(s2 == 0.0, 0.0, s2 * lax.rsqrt(s2)) * (gi * xc)

    a = a.reshape(tt // SUBLANES, SUBLANES, tc)
    u = u.reshape(tt // SUBLANES, SUBLANES, tc)
    rin = lax.broadcasted_iota(jnp.int32, a.shape, 1)
    s = 1
    while s < SUBLANES:
        keep = rin >= s
        a_sh = jnp.where(keep, pltpu.roll(a, s, axis=1), 1.0)
        u_sh = jnp.where(keep, pltpu.roll(u, s, axis=1), 0.0)
        u = a * u_sh + u
        a = a * a_sh
        s *= 2
    a = a.reshape(tt, tc)
    u = u.reshape(tt, tc)
    carry = hcar[...]
    tiles = []
    for i in range(tt // SUBLANES):
        rows = slice(i * SUBLANES, (i + 1) * SUBLANES)
        h_i = a[rows] * carry + u[rows]
        carry = h_i[SUBLANES - 1:SUBLANES, :]
        tiles.append(h_i)
    h = jnp.concatenate(tiles, axis=0)
    hcar[...] = carry

    ga = ga_ref[...]
    o_ref[...] = (h * (ga * _sigmoid(ga))).astype(o_ref.dtype)


def _rglru(p, bsz, seq, d_lru, cw, cb, w_r, b_r, w_i, b_i, lam, tt=512, tc=1024):
    tt = min(tt, seq)
    tc = min(tc, d_lru)
    nt, nc = seq // tt, d_lru // tc
    hpb = tc // HEAD_DIM
    row = lambda v: v.reshape(1, d_lru)
    vec_spec = pl.BlockSpec((1, tc), lambda b, j, t: (0, j))
    return pl.pallas_call(
        functools.partial(_rglru_kernel, tt=tt, tc=tc),
        grid=(bsz, nc, nt),
        in_specs=[
            pl.BlockSpec((tt, tc), lambda b, j, t: (b * nt + t, j)),
            pl.BlockSpec((tt, tc), lambda b, j, t: (b * nt + t, nc + j)),
            pl.BlockSpec((CONV_W, tc), lambda b, j, t: (0, j)),
            vec_spec,
            pl.BlockSpec((hpb, HEAD_DIM, HEAD_DIM), lambda b, j, t: (j, 0, 0)),
            vec_spec,
            pl.BlockSpec((hpb, HEAD_DIM, HEAD_DIM), lambda b, j, t: (j, 0, 0)),
            vec_spec,
            vec_spec,
        ],
        out_specs=pl.BlockSpec((tt, tc), lambda b, j, t: (b * nt + t, j)),
        out_shape=jax.ShapeDtypeStruct((bsz * seq, d_lru), BF16),
        scratch_shapes=[pltpu.VMEM((HIST, tc), F32), pltpu.VMEM((1, tc), F32)],
        compiler_params=_params(("arbitrary", "arbitrary", "arbitrary")),
        name="rglru",
    )(p, p, cw, row(cb), w_r, row(b_r), w_i, row(b_i), row(lam))


def _norm_gates_kernel(x_ref, w_ref, wt_ref, alog_ref, dtb_ref, h_ref, g_ref, wt_bf,
                       *, nheads, tm):
    @pl.when(pl.program_id(0) == 0)
    def _():
        wt_bf[...] = wt_ref[...].astype(BF16)

    x = x_ref[...]
    ms = jnp.mean(x * x, axis=-1, keepdims=True)
    h = ((x * lax.rsqrt(ms + EPS)) * w_ref[...]).astype(BF16)
    h_ref[...] = h
    tail = lax.dot_general(h, wt_bf[...], _NT, preferred_element_type=F32)

    lane = lax.broadcasted_iota(jnp.int32, (tm, LANES), 1)
    is_a = (lane >= nheads) & (lane < 2 * nheads)
    g = jnp.where(is_a, -jnp.exp(alog_ref[...]) * _softplus(tail + dtb_ref[...]), 0.0)
    rmod = lax.broadcasted_iota(jnp.int32, (tm, LANES), 0) & (CHUNK - 1)
    s = 1
    while s < CHUNK:
        g = g + jnp.where(rmod >= s, pltpu.roll(g, s, axis=0), 0.0)
        s *= 2
    g_ref[...] = jnp.where(is_a, g, _sigmoid(tail))


def _norm_gates(x2, w, w_tail_nk, a_log, dt_bias, nheads, tm=512):
    m, d = x2.shape
    tm = min(tm, m)
    assert tm % CHUNK == 0
    pad_row = lambda v: jnp.pad(v, (nheads, LANES - 2 * nheads)).reshape(1, LANES)
    const = lambda shape: pl.BlockSpec(shape, lambda i: (0, 0))
    return pl.pallas_call(
        functools.partial(_norm_gates_kernel, nheads=nheads, tm=tm),
        grid=(m // tm,),
        in_specs=[pl.BlockSpec((tm, d), lambda i: (i, 0)), const((1, d)), const((LANES, d)),
                  const((1, LANES)), const((1, LANES))],
        out_specs=[pl.BlockSpec((tm, d), lambda i: (i, 0)),
                   pl.BlockSpec((tm, LANES), lambda i: (i, 0))],
        out_shape=[jax.ShapeDtypeStruct((m, d), BF16), jax.ShapeDtypeStruct((m, LANES), F32)],
        scratch_shapes=[pltpu.VMEM((LANES, d), BF16)],
        compiler_params=_params(("arbitrary",)),
        name="norm_gates",
    )(x2, w.reshape(1, d), w_tail_nk, pad_row(a_log), pad_row(dt_bias))


def _dn_prep_kernel(q_ref, k_ref, v_ref, gate_ref, cwq_ref, cwk_ref, cwv_ref,
                    u_ref, wq_ref, kdt_ref, attn_ref, gl_ref,
                    qbuf, kbuf, vbuf, *, tt, hg, nheads):
    hp = pl.program_id(1)
    first = pl.program_id(2) == 0

    @pl.when(first)
    def _():
        for buf in (qbuf, kbuf, vbuf):
            buf[...] = jnp.zeros_like(buf)

    def conv_silu(ref, buf, cw_ref, lo, cs):
        if lo == 0:
            ext = jnp.concatenate([buf[:, cs], ref[0:GROUP, cs]], axis=0)
        else:
            ext = ref[lo - HIST:lo + GROUP, cs]
        z = cw_ref[0:1, cs] * ext
        for kk in range(1, CONV_W):
            z = cw_ref[kk:kk + 1, cs] * ext + pltpu.roll(z, 1, axis=0)
        y = z[HIST:HIST + GROUP, :]
        return y * _sigmoid(y)

    lane = lax.broadcasted_iota(jnp.int32, (GROUP, LANES), 1)
    ii = lax.broadcasted_iota(jnp.int32, (GROUP, GROUP), 0)
    jj = lax.broadcasted_iota(jnp.int32, (GROUP, GROUP), 1)
    same_chunk = (jnp.bitwise_xor(ii, jj) & CHUNK) == 0
    causal = same_chunk & (ii >= jj)
    strict = same_chunk & (ii > jj)
    eye = jnp.where(ii == jj, 1.0, 0.0)
    top = ii < CHUNK

    chains = []
    for hh in range(hg):
        hd = hp * hg + hh
        cs = slice(hh * HEAD_DIM, (hh + 1) * HEAD_DIM)
        for g in range(tt // GROUP):
            lo = g * GROUP
            rs = slice(lo, lo + GROUP)
            q = conv_silu(q_ref, qbuf, cwq_ref, lo, cs)
            k = conv_silu(k_ref, kbuf, cwk_ref, lo, cs)
            v = conv_silu(v_ref, vbuf, cwv_ref, lo, cs)
            q = (q * lax.rsqrt(jnp.sum(q * q, axis=-1, keepdims=True) + EPS)) * (HEAD_DIM ** -0.5)
            k = k * lax.rsqrt(jnp.sum(k * k, axis=-1, keepdims=True) + EPS)
            gate = gate_ref[rs, :]
            beta = jnp.broadcast_to(
                jnp.sum(jnp.where(lane == hd, gate, 0.0), axis=-1, keepdims=True), (GROUP, LANES))
            gc = jnp.broadcast_to(
                jnp.sum(jnp.where(lane == hd + nheads, gate, 0.0), axis=-1, keepdims=True),
                (GROUP, LANES))
            chains.append(dict(cs=cs, g=g, rs=rs, q=q, k=k, v=v, beta=beta, gc=gc,
                               egc=jnp.exp(gc)))
    for ref, buf in ((q_ref, qbuf), (k_ref, kbuf), (v_ref, vbuf)):
        buf[...] = ref[tt - HIST:tt, :]

    for ch in chains:
        gcg = ch["gc"]
        ch["decay"] = jnp.exp(jnp.where(causal, gcg - gcg.T, -jnp.inf))
        ch["kb"] = ch["k"] * ch["beta"]
    for ch in chains:
        ch["kk"] = _dot(ch["kb"], ch["k"], _NT)
    for ch in chains:
        ch["qk"] = _dot(ch["q"], ch["k"], _NT)
    for ch in chains:
        ch["p"] = -jnp.where(strict, ch["kk"] * ch["decay"], 0.0)
        ch["t"] = eye + ch["p"]
    pw = 2
    while pw < CHUNK:
        for ch in chains:
            ch["p"] = _dot(ch["p"], ch["p"])
        for ch in chains:
            ch["t"] = ch["t"] + _dot(ch["p"], ch["t"])
        pw *= 2
    for ch in chains:
        rhs = jnp.concatenate([ch["v"] * ch["beta"], ch["kb"] * ch["egc"]], axis=1)
        ch["sol"] = _dot(ch["t"], rhs)

    for ch in chains:
        cs, g, rs, gcg = ch["cs"], ch["g"], ch["rs"], ch["gc"]
        lo = g * GROUP
        u = ch["sol"][:, 0:HEAD_DIM]
        w = ch["sol"][:, HEAD_DIM:2 * HEAD_DIM].astype(BF16)
        attn = ch["qk"] * ch["decay"]
        q_dec = (ch["q"] * ch["egc"]).astype(BF16)
        gl0 = gcg[CHUNK - 1:CHUNK, :]
        gl1 = gcg[GROUP - 1:GROUP, :]
        k_dec = ch["k"] * jnp.exp(jnp.where(top, gl0, gl1) - gcg)

        u_ref[rs, cs] = u
        for c in range(2):
            cr = slice(c * CHUNK, (c + 1) * CHUNK)
            wq_ref[2 * lo + c * GROUP:2 * lo + c * GROUP + CHUNK, cs] = w[cr]
            wq_ref[2 * lo + c * GROUP + CHUNK:2 * lo + (c + 1) * GROUP, cs] = q_dec[cr]
        kdt_ref[rs, cs] = k_dec.T.astype(BF16)
        attn_ref[rs, cs] = attn.astype(BF16)
        gl_ref[2 * g:2 * g + 1, cs] = jnp.exp(gl0)
        gl_ref[2 * g + 1:2 * g + 2, cs] = jnp.exp(gl1)


def _dn_scan_kernel(u_ref, wq_ref, kdt_ref, attn_ref, gl_ref, z_ref, nw_ref, o_ref, s_ref,
                    *, tt, hg):
    @pl.when(pl.program_id(2) == 0)
    def _():
        s_ref[...] = jnp.zeros_like(s_ref)

    nw = nw_ref[...]
    zeros = jnp.zeros((CHUNK, HEAD_DIM), F32)
    cols = [slice(hh * HEAD_DIM, (hh + 1) * HEAD_DIM) for hh in range(hg)]
    s_mats = [s_ref[hh] for hh in range(hg)]
    for g in range(tt // GROUP):
        lo = g * GROUP
        rs = slice(lo, lo + GROUP)
        v_prev = [zeros] * hg
        outs = [[] for _ in range(hg)]
        for c in range(2):
            r = [_dot(wq_ref[2 * lo + c * GROUP:2 * lo + (c + 1) * GROUP, cs], s_mats[hh])
                 for hh, cs in enumerate(cols)]
            v_new = [u_ref[lo + c * CHUNK:lo + (c + 1) * CHUNK, cs] - r[hh][0:CHUNK]
                     for hh, cs in enumerate(cols)]
            for hh, cs in enumerate(cols):
                v_upd = jnp.concatenate([zeros, v_new[hh]] if c else [v_new[hh], zeros], axis=0)
                s_mats[hh] = (s_mats[hh] * gl_ref[2 * g + c:2 * g + c + 1, cs]
                              + _dot(kdt_ref[rs, cs], v_upd))
            for hh, cs in enumerate(cols):
                v_att = jnp.concatenate([v_prev[hh], v_new[hh]] if c else [v_new[hh], zeros], axis=0)
                outs[hh].append(r[hh][CHUNK:GROUP]
                                + _dot(attn_ref[lo + c * CHUNK:lo + (c + 1) * CHUNK, cs], v_att))
            v_prev = v_new
        for hh, cs in enumerate(cols):
            o = jnp.concatenate(outs[hh], axis=0)
            ms = jnp.mean(o * o, axis=-1, keepdims=True)
            z = z_ref[rs, cs]
            o_ref[rs, cs] = (((o * lax.rsqrt(ms + EPS)) * nw) * (z * _sigmoid(z))).astype(o_ref.dtype)
    for hh in range(hg):
        s_ref[hh] = s_mats[hh]


def _deltanet(p, gates, bsz, seq, d_lru, d_dn, conv_w3, layer, norm_w, tt_prep=1024, tt=512,
              hg_prep=4, hg_scan=16):
    nheads = d_dn // HEAD_DIM
    m = bsz * seq
    tt = min(tt, seq)
    tt_prep = min(tt_prep, seq)
    hg_prep = min(hg_prep, nheads)
    hg_scan = min(hg_scan, nheads)
    nt = seq // tt
    nt_prep = seq // tt_prep
    base = 2 * d_lru // HEAD_DIM
    ncb = 2 * tt // GROUP

    def prep_cols(off):
        return pl.BlockSpec((tt_prep, hg_prep * HEAD_DIM),
                            lambda b, h, t: (b * nt_prep + t, (base + off * nheads) // hg_prep + h))

    def prep_cw(off):
        return pl.BlockSpec((None, CONV_W, hg_prep * HEAD_DIM),
                            lambda b, h, t: (layer, 0, off * nheads // hg_prep + h))

    assert base % hg_prep == 0 and nheads % hg_prep == 0 and nheads % hg_scan == 0
    wide = hg_prep * HEAD_DIM
    row_blk = lambda rows: pl.BlockSpec((rows, wide), lambda b, h, t: (b * nt_prep + t, h))
    u, wq, kdt, attn, gl = pl.pallas_call(
        functools.partial(_dn_prep_kernel, tt=tt_prep, hg=hg_prep, nheads=nheads),
        grid=(bsz, nheads // hg_prep, nt_prep),
        in_specs=[prep_cols(0), prep_cols(1), prep_cols(2),
                  pl.BlockSpec((tt_prep, LANES), lambda b, h, t: (b * nt_prep + t, 0)),
                  prep_cw(0), prep_cw(1), prep_cw(2)],
        out_specs=[row_blk(tt_prep), row_blk(2 * tt_prep), row_blk(tt_prep), row_blk(tt_prep),
                   row_blk(tt_prep // CHUNK)],
        out_shape=[jax.ShapeDtypeStruct((m, d_dn), F32),
                   jax.ShapeDtypeStruct((2 * m, d_dn), BF16),
                   jax.ShapeDtypeStruct((m, d_dn), BF16),
                   jax.ShapeDtypeStruct((m, d_dn), BF16),
                   jax.ShapeDtypeStruct((m // CHUNK, d_dn), F32)],
        scratch_shapes=[pltpu.VMEM((HIST, wide), F32)] * 3,
        compiler_params=_params(("arbitrary", "arbitrary", "arbitrary")),
        name="dn_prep",
    )(p, p, p, gates, conv_w3, conv_w3, conv_w3)

    wide = hg_scan * HEAD_DIM
    row_blk = lambda rows: pl.BlockSpec((rows, wide), lambda b, h, t: (b * nt + t, h))
    z_off = (base + 3 * nheads) // hg_scan
    assert (base + 3 * nheads) % hg_scan == 0
    return pl.pallas_call(
        functools.partial(_dn_scan_kernel, tt=tt, hg=hg_scan),
        grid=(bsz, nheads // hg_scan, nt),
        in_specs=[row_blk(tt), row_blk(2 * tt), row_blk(tt), row_blk(tt), row_blk(ncb),
                  pl.BlockSpec((tt, wide), lambda b, h, t: (b * nt + t, z_off + h)),
                  pl.BlockSpec((1, HEAD_DIM), lambda b, h, t: (0, 0))],
        out_specs=row_blk(tt),
        out_shape=jax.ShapeDtypeStruct((m, d_dn), BF16),
        scratch_shapes=[pltpu.VMEM((hg_scan, HEAD_DIM, HEAD_DIM), F32)],
        compiler_params=_params(("arbitrary", "arbitrary", "arbitrary")),
        name="dn_scan",
    )(u, wq, kdt, attn, gl, p, norm_w.reshape(1, HEAD_DIM))


def _stream_kernel(*refs, n_a, n_w, w_is_nk, shortconv, has_res, mi, nj, nb, tm, tn, ks, layer,
                   blocks_per_seq):
    a_refs, w_hbm = refs[:n_a], refs[n_a]
    rest = list(refs[n_a + 1:])
    cw_ref = rest.pop(0) if shortconv else None
    r_ref = rest.pop(0) if has_res else None
    o_ref, wbf0, wbf1, stage, sem = rest[:5]
    hist = rest[5] if shortconv else None
    wbf = (wbf0, wbf1)
    j = pl.program_id(0)
    i = pl.program_id(1)
    lin = j * mi + i
    n_steps = nj * mi

    def slab_copies(l, slot):
        jb = lax.rem(lax.div(l, mi), nj)
        r = lax.rem(l, mi)
        out = []
        for n in range(n_w):
            if w_is_nk:
                src = w_hbm.at[layer, pl.ds(jb * tn + r * ks, ks), :]
            else:
                src = w_hbm.at[layer, pl.ds(r * ks, ks), pl.ds((n * nb + jb) * tn, tn)]
            out.append(pltpu.make_async_copy(src, stage.at[slot, n], sem.at[slot, n]))
        return out

    def cast_slab(slot, r, dst):
        rows = pl.ds(pl.multiple_of(r * ks, ks), ks)
        for n in range(n_w):
            cols = slice(None) if w_is_nk else slice(n * tn, (n + 1) * tn)
            dst[rows, cols] = stage[slot, n].astype(BF16)

    @pl.when(lin == 0)
    def _():
        for c in slab_copies(0, 0):
            c.start()
        for r in range(mi):
            for c in slab_copies(r + 1, (r + 1) % 2):
                c.start()
            for c in slab_copies(r, r % 2):
                c.wait()
            cast_slab(r % 2, r, wbf[0])

    def body(cur, oth):
        slot = lax.rem(i, 2)
        for c in slab_copies(lin + mi, slot):
            c.wait()
        for c in slab_copies(lin + mi + 1, 1 - slot):
            c.start()
        cast_slab(slot, i, oth)
        proj = None
        off = 0
        for a_ref in a_refs:
            kk = a_ref.shape[1]
            if w_is_nk:
                part = lax.dot_general(a_ref[...], cur[:, off:off + kk], _NT,
                                       preferred_element_type=F32)
            else:
                part = jnp.dot(a_ref[...], cur[off:off + kk, :], preferred_element_type=F32)
            proj = part if proj is None else proj + part
            off += kk
        if has_res:
            proj = proj + r_ref[...]
        if shortconv:
            xin, gate_b, gate_c, z = (proj[:, n * tn:(n + 1) * tn] for n in range(4))
            conv = _causal_conv(gate_c * xin, hist, cw_ref[...], SC_CONV_W, tm,
                                lax.rem(i, blocks_per_seq) == 0)
            proj = (gate_b * conv) * (z * _sigmoid(z))
        o_ref[...] = proj.astype(o_ref.dtype)

    @pl.when(lax.rem(j, 2) == 0)
    def _():
        body(wbf[0], wbf[1])

    @pl.when(lax.rem(j, 2) == 1)
    def _():
        body(wbf[1], wbf[0])

    @pl.when(lin == n_steps - 1)
    def _():
        for c in slab_copies(lin + mi + 1, 1 - lax.rem(i, 2)):
            c.wait()


def _stream_matmul(a_parts, w3, layer, n_cols, *, w_is_nk=False, conv_w3=None, seq=None, res=None,
                   out_dtype=F32, tm=1024, tn=1024, name="stream_matmul"):
    m = a_parts[0].shape[0]
    k = sum(a.shape[1] for a in a_parts)
    shortconv = conv_w3 is not None
    n_w = 4 if shortconv else 1
    tm = min(tm, seq if shortconv else m)
    if (m // tm) % 2:
        tm //= 2
    tn = max(t for t in range(LANES, min(tn, n_cols) + 1, LANES) if n_cols % t == 0)
    mi, nj = m // tm, n_cols // tn
    assert m % tm == 0 and mi % 2 == 0 and not (shortconv and w_is_nk)
    if w_is_nk:
        ks = tn // mi
        wbf_shape, stage_shape = (tn, k), (2, 1, ks, k)
        assert tn % mi == 0 and ks % 16 == 0
    else:
        ks = k // mi
        wbf_shape, stage_shape = (k, n_w * tn), (2, n_w, ks, tn)
        assert k % mi == 0 and ks % 16 == 0
    in_specs = [pl.BlockSpec((tm, a.shape[1]), lambda j, i: (i, 0)) for a in a_parts]
    in_specs.append(pl.BlockSpec(memory_space=pl.ANY))
    args = list(a_parts) + [w3]
    scratch = [pltpu.VMEM(wbf_shape, BF16), pltpu.VMEM(wbf_shape, BF16),
               pltpu.VMEM(stage_shape, F32), pltpu.SemaphoreType.DMA((2, n_w))]
    if shortconv:
        in_specs.append(pl.BlockSpec((None, SC_CONV_W, tn), lambda j, i: (layer, 0, j)))
        args.append(conv_w3)
        scratch.append(pltpu.VMEM((HIST, tn), F32))
    if res is not None:
        in_specs.append(pl.BlockSpec((tm, tn), lambda j, i: (i, j)))
        args.append(res)
    return pl.pallas_call(
        functools.partial(_stream_kernel, n_a=len(a_parts), n_w=n_w, w_is_nk=w_is_nk,
                          shortconv=shortconv, has_res=res is not None, mi=mi,
                          nj=nj, nb=nj, tm=tm, tn=tn, ks=ks, layer=layer,
                          blocks_per_seq=(seq // tm) if shortconv else 1),
        grid=(nj, mi),
        in_specs=in_specs,
        out_specs=pl.BlockSpec((tm, tn), lambda j, i: (i, j)),
        out_shape=jax.ShapeDtypeStruct((m, n_cols), out_dtype),
        scratch_shapes=scratch,
        compiler_params=_params(("arbitrary", "arbitrary"),
                                VMEM_LIMIT_RES if res is not None else VMEM_LIMIT),
        name=name,
    )(*args)


def kernel(x, even_norm_w, even_w_in, lru_conv_w, lru_conv_b, lru_w_r, lru_b_r, lru_w_i, lru_b_i,
           lru_lambda, dn_conv_w, dn_a_log, dn_dt_bias, dn_norm_w, even_w_out, odd_norm_w,
           odd_w_in, odd_conv_w, odd_w_out, final_norm_w):
    bsz, seq, d = x.shape
    x2 = x.reshape(bsz * seq, d)
    d_lru = lru_lambda.shape[1]
    d_dn = dn_conv_w.shape[2] // 3
    nheads = d_dn // HEAD_DIM
    n_main = 2 * d_lru + 4 * d_dn
    depth = even_norm_w.shape[0] + odd_norm_w.shape[0]
    for layer in range(depth):
        j = layer // 2
        if layer % 2 == 0:
            w_in_nk = jnp.swapaxes(even_w_in, 1, 2)
            w_tail = jnp.pad(w_in_nk[j, n_main:, :], ((0, LANES - 2 * nheads), (0, 0)))
            h, gates = _norm_gates(x2, even_norm_w[j], w_tail, dn_a_log[j], dn_dt_bias[j], nheads)
            p = _stream_matmul([h], w_in_nk, j, n_main, w_is_nk=True, name="in_proj0")
            ya = _rglru(p, bsz, seq, d_lru, lru_conv_w[j], lru_conv_b[j], lru_w_r[j], lru_b_r[j],
                        lru_w_i[j], lru_b_i[j], lru_lambda[j])
            yb = _deltanet(p, gates, bsz, seq, d_lru, d_dn, dn_conv_w, j, dn_norm_w[j])
            x2 = _stream_matmul([ya, yb], even_w_out, j, d, res=x2, name="out_proj0")
        else:
            h = _rmsnorm(x2, odd_norm_w[j], BF16)
            y = _stream_matmul([h], odd_w_in, j, odd_conv_w.shape[2], conv_w3=odd_conv_w, seq=seq,
                               out_dtype=BF16, tn=SC_TN, name="shortconv")
            x2 = _stream_matmul([y], odd_w_out, j, d, res=x2, name="out_proj1")
    return _rmsnorm(x2, final_norm_w, x.dtype).reshape(bsz, seq, d)
```

```python
import functools

import jax
import jax.numpy as jnp
from jax import lax
from jax.experimental import pallas as pl
from jax.experimental.pallas import tpu as pltpu

HEAD_DIM = 128
CHUNK = 64
GROUP = 2 * CHUNK
LANES = 128
SUBLANES = 8
CONV_W = 4
SC_CONV_W = 3
LRU_C = 8.0
EPS = 1e-6
HIST = 8
VMEM_LIMIT = 56 * 1024 * 1024
VMEM_LIMIT_RES = 61 * 1024 * 1024
SC_TN = 256

F32 = jnp.float32
BF16 = jnp.bfloat16

assert GROUP == HEAD_DIM == LANES


def _params(sem, vmem=VMEM_LIMIT):
    return pltpu.CompilerParams(dimension_semantics=sem, vmem_limit_bytes=vmem)


def _sigmoid(x):
    return jax.nn.sigmoid(x)


def _softplus(x):
    return jnp.maximum(x, 0.0) + jnp.log1p(jnp.exp(-jnp.abs(x)))


def _dot(a, b, dims=(((1,), (0,)), ((), ()))):
    return lax.dot_general(a.astype(BF16), b.astype(BF16), dims, preferred_element_type=F32)


_NT = (((1,), (1,)), ((), ()))


def _rmsnorm_kernel(x_ref, w_ref, o_ref):
    x = x_ref[...]
    ms = jnp.mean(x * x, axis=-1, keepdims=True)
    o_ref[...] = ((x * lax.rsqrt(ms + EPS)) * w_ref[...]).astype(o_ref.dtype)


def _rmsnorm(x2, w, out_dtype, tm=512):
    m, d = x2.shape
    tm = min(tm, m)
    return pl.pallas_call(
        _rmsnorm_kernel,
        grid=(m // tm,),
        in_specs=[pl.BlockSpec((tm, d), lambda i: (i, 0)),
                  pl.BlockSpec((1, d), lambda i: (0, 0))],
        out_specs=pl.BlockSpec((tm, d), lambda i: (i, 0)),
        out_shape=jax.ShapeDtypeStruct((m, d), out_dtype),
        compiler_params=_params(("arbitrary",)),
        name="rmsnorm",
    )(x2, w.reshape(1, d))


def _causal_conv(x_raw, hist, cw, width, tt, first):
    ext = jnp.concatenate([jnp.where(first, 0.0, hist[...]), x_raw], axis=0)
    hist[...] = x_raw[tt - HIST:tt, :]
    z = cw[0:1, :] * ext
    for kk in range(1, width):
        z = cw[kk:kk + 1, :] * ext + pltpu.roll(z, 1, axis=0)
    return z[HIST:HIST + tt, :]


def _rglru_kernel(xa_ref, ga_ref, cw_ref, cb_ref, wr_ref, br_ref, wi_ref, bi_ref, lam_ref,
                  o_ref, xbuf, hcar, *, tt, tc):
    t = pl.program_id(2)

    @pl.when(t == 0)
    def _():
        hcar[...] = jnp.zeros_like(hcar)

    xc = _causal_conv(xa_ref[...], xbuf, cw_ref[...], CONV_W, tt, t == 0) + cb_ref[...]

    xcb = xc.astype(BF16)
    r_parts, i_parts = [], []
    for hh in range(tc // HEAD_DIM):
        xh = xcb[:, hh * HEAD_DIM:(hh + 1) * HEAD_DIM]
        r_parts.append(jnp.dot(xh, wr_ref[hh].astype(BF16), preferred_element_type=F32))
        i_parts.append(jnp.dot(xh, wi_ref[hh].astype(BF16), preferred_element_type=F32))
    r = _sigmoid(jnp.concatenate(r_parts, axis=1) + br_ref[...])
    gi = _sigmoid(jnp.concatenate(i_parts, axis=1) + bi_ref[...])

    log_a = (-LRU_C) * r * _softplus(-lam_ref[...])
    a = jnp.exp(log_a)
    s2 = -jnp.tanh(log_a) * (a * a + 1.0)
    u = jnp.where(s2 == 0.0, 0.0, s2 * lax.rsqrt(s2)) * (gi * xc)

    a = a.reshape(tt // SUBLANES, SUBLANES, tc)
    u = u.reshape(tt // SUBLANES, SUBLANES, tc)
    rin = lax.broadcasted_iota(jnp.int32, a.shape, 1)
    s = 1
    while s < SUBLANES:
        keep = rin >= s
        a_sh = jnp.where(keep, pltpu.roll(a, s, axis=1), 1.0)
        u_sh = jnp.where(keep, pltpu.roll(u, s, axis=1), 0.0)
        u = a * u_sh + u
        a = a * a_sh
        s *= 2
    a = a.reshape(tt, tc)
    u = u.reshape(tt, tc)
    carry = hcar[...]
    tiles = []
    for i in range(tt // SUBLANES):
        rows = slice(i * SUBLANES, (i + 1) * SUBLANES)
        h_i = a[rows] * carry + u[rows]
        carry = h_i[SUBLANES - 1:SUBLANES, :]
        tiles.append(h_i)
    h = jnp.concatenate(tiles, axis=0)
    hcar[...] = carry

    ga = ga_ref[...]
    o_ref[...] = (h * (ga * _sigmoid(ga))).astype(o_ref.dtype)


def _rglru(p, bsz, seq, d_lru, cw, cb, w_r, b_r, w_i, b_i, lam, tt=512, tc=1024):
    tt = min(tt, seq)
    tc = min(tc, d_lru)
    nt, nc = seq // tt, d_lru // tc
    hpb = tc // HEAD_DIM
    row = lambda v: v.reshape(1, d_lru)
    vec_spec = pl.BlockSpec((1, tc), lambda b, j, t: (0, j))
    return pl.pallas_call(
        functools.partial(_rglru_kernel, tt=tt, tc=tc),
        grid=(bsz, nc, nt),
        in_specs=[
            pl.BlockSpec((tt, tc), lambda b, j, t: (b * nt + t, j)),
            pl.BlockSpec((tt, tc), lambda b, j, t: (b * nt + t, nc + j)),
            pl.BlockSpec((CONV_W, tc), lambda b, j, t: (0, j)),
            vec_spec,
            pl.BlockSpec((hpb, HEAD_DIM, HEAD_DIM), lambda b, j, t: (j, 0, 0)),
            vec_spec,
            pl.BlockSpec((hpb, HEAD_DIM, HEAD_DIM), lambda b, j, t: (j, 0, 0)),
            vec_spec,
            vec_spec,
        ],
        out_specs=pl.BlockSpec((tt, tc), lambda b, j, t: (b * nt + t, j)),
        out_shape=jax.ShapeDtypeStruct((bsz * seq, d_lru), BF16),
        scratch_shapes=[pltpu.VMEM((HIST, tc), F32), pltpu.VMEM((1, tc), F32)],
        compiler_params=_params(("arbitrary", "arbitrary", "arbitrary")),
        name="rglru",
    )(p, p, cw, row(cb), w_r, row(b_r), w_i, row(b_i), row(lam))


def _norm_gates_kernel(x_ref, w_ref, wt_ref, alog_ref, dtb_ref, h_ref, g_ref, wt_bf,
                       *, nheads, tm):
    @pl.when(pl.program_id(0) == 0)
    def _():
        wt_bf[...] = wt_ref[...].astype(BF16)

    x = x_ref[...]
    ms = jnp.mean(x * x, axis=-1, keepdims=True)
    h = ((x * lax.rsqrt(ms + EPS)) * w_ref[...]).astype(BF16)
    h_ref[...] = h
    tail = lax.dot_general(h, wt_bf[...], _NT, preferred_element_type=F32)

    lane = lax.broadcasted_iota(jnp.int32, (tm, LANES), 1)
    is_a = (lane >= nheads) & (lane < 2 * nheads)
    g = jnp.where(is_a, -jnp.exp(alog_ref[...]) * _softplus(tail + dtb_ref[...]), 0.0)
    rmod = lax.broadcasted_iota(jnp.int32, (tm, LANES), 0) & (CHUNK - 1)
    s = 1
    while s < CHUNK:
        g = g + jnp.where(rmod >= s, pltpu.roll(g, s, axis=0), 0.0)
        s *= 2
    g_ref[...] = jnp.where(is_a, g, _sigmoid(tail))


def _norm_gates(x2, w, w_tail_nk, a_log, dt_bias, nheads, tm=512):
    m, d = x2.shape
    tm = min(tm, m)
    assert tm % CHUNK == 0
    pad_row = lambda v: jnp.pad(v, (nheads, LANES - 2 * nheads)).reshape(1, LANES)
    const = lambda shape: pl.BlockSpec(shape, lambda i: (0, 0))
    return pl.pallas_call(
        functools.partial(_norm_gates_kernel, nheads=nheads, tm=tm),
        grid=(m // tm,),
        in_specs=[pl.BlockSpec((tm, d), lambda i: (i, 0)), const((1, d)), const((LANES, d)),
                  const((1, LANES)), const((1, LANES))],
        out_specs=[pl.BlockSpec((tm, d), lambda i: (i, 0)),
                   pl.BlockSpec((tm, LANES), lambda i: (i, 0))],
        out_shape=[jax.ShapeDtypeStruct((m, d), BF16), jax.ShapeDtypeStruct((m, LANES), F32)],
        scratch_shapes=[pltpu.VMEM((LANES, d), BF16)],
        compiler_params=_params(("arbitrary",)),
        name="norm_gates",
    )(x2, w.reshape(1, d), w_tail_nk, pad_row(a_log), pad_row(dt_bias))


def _dn_prep_kernel(q_ref, k_ref, v_ref, gate_ref, cwq_ref, cwk_ref, cwv_ref,
                    u_ref, wq_ref, kdt_ref, attn_ref, gl_ref,
                    qbuf, kbuf, vbuf, *, tt, hg, nheads):
    hp = pl.program_id(1)
    first = pl.program_id(2) == 0

    @pl.when(first)
    def _():
        for buf in (qbuf, kbuf, vbuf):
            buf[...] = jnp.zeros_like(buf)

    def conv_silu(ref, buf, cw_ref, lo, cs):
        if lo == 0:
            ext = jnp.concatenate([buf[:, cs], ref[0:GROUP, cs]], axis=0)
        else:
            ext = ref[lo - HIST:lo + GROUP, cs]
        z = cw_ref[0:1, cs] * ext
        for kk in range(1, CONV_W):
            z = cw_ref[kk:kk + 1, cs] * ext + pltpu.roll(z, 1, axis=0)
        y = z[HIST:HIST + GROUP, :]
        return y * _sigmoid(y)

    lane = lax.broadcasted_iota(jnp.int32, (GROUP, LANES), 1)
    ii = lax.broadcasted_iota(jnp.int32, (GROUP, GROUP), 0)
    jj = lax.broadcasted_iota(jnp.int32, (GROUP, GROUP), 1)
    same_chunk = (jnp.bitwise_xor(ii, jj) & CHUNK) == 0
    causal = same_chunk & (ii >= jj)
    strict = same_chunk & (ii > jj)
    eye = jnp.where(ii == jj, 1.0, 0.0)
    top = ii < CHUNK

    chains = []
    for hh in range(hg):
        hd = hp * hg + hh
        cs = slice(hh * HEAD_DIM, (hh + 1) * HEAD_DIM)
        for g in range(tt // GROUP):
            lo = g * GROUP
            rs = slice(lo, lo + GROUP)
            q = conv_silu(q_ref, qbuf, cwq_ref, lo, cs)
            k = conv_silu(k_ref, kbuf, cwk_ref, lo, cs)
            v = conv_silu(v_ref, vbuf, cwv_ref, lo, cs)
            q = (q * lax.rsqrt(jnp.sum(q * q, axis=-1, keepdims=True) + EPS)) * (HEAD_DIM ** -0.5)
            k = k * lax.rsqrt(jnp.sum(k * k, axis=-1, keepdims=True) + EPS)
            gate = gate_ref[rs, :]
            beta = jnp.broadcast_to(
                jnp.sum(jnp.where(lane == hd, gate, 0.0), axis=-1, keepdims=True), (GROUP, LANES))
            gc = jnp.broadcast_to(
                jnp.sum(jnp.where(lane == hd + nheads, gate, 0.0), axis=-1, keepdims=True),
                (GROUP, LANES))
            chains.append(dict(cs=cs, g=g, rs=rs, q=q, k=k, v=v, beta=beta, gc=gc,
                               egc=jnp.exp(gc)))
    for ref, buf in ((q_ref, qbuf), (k_ref, kbuf), (v_ref, vbuf)):
        buf[...] = ref[tt - HIST:tt, :]

    for ch in chains:
        gcg = ch["gc"]
        ch["decay"] = jnp.exp(jnp.where(causal, gcg - gcg.T, -jnp.inf))
        ch["kb"] = ch["k"] * ch["beta"]
    for ch in chains:
        ch["kk"] = _dot(ch["kb"], ch["k"], _NT)
    for ch in chains:
        ch["qk"] = _dot(ch["q"], ch["k"], _NT)
    for ch in chains:
        ch["p"] = -jnp.where(strict, ch["kk"] * ch["decay"], 0.0)
        ch["t"] = eye + ch["p"]
    pw = 2
    while pw < CHUNK:
        for ch in chains:
            ch["p"] = _dot(ch["p"], ch["p"])
        for ch in chains:
            ch["t"] = ch["t"] + _dot(ch["p"], ch["t"])
        pw *= 2
    for ch in chains:
        rhs = jnp.concatenate([ch["v"] * ch["beta"], ch["kb"] * ch["egc"]], axis=1)
        ch["sol"] = _dot(ch["t"], rhs)

    for ch in chains:
        cs, g, rs, gcg = ch["cs"], ch["g"], ch["rs"], ch["gc"]
        lo = g * GROUP
        u = ch["sol"][:, 0:HEAD_DIM]
        w = ch["sol"][:, HEAD_DIM:2 * HEAD_DIM].astype(BF16)
        attn = ch["qk"] * ch["decay"]
        q_dec = (ch["q"] * ch["egc"]).astype(BF16)
        gl0 = gcg[CHUNK - 1:CHUNK, :]
        gl1 = gcg[GROUP - 1:GROUP, :]
        k_dec = ch["k"] * jnp.exp(jnp.where(top, gl0, gl1) - gcg)

        u_ref[rs, cs] = u
        for c in range(2):
            cr = slice(c * CHUNK, (c + 1) * CHUNK)
            wq_ref[2 * lo + c * GROUP:2 * lo + c * GROUP + CHUNK, cs] = w[cr]
            wq_ref[2 * lo + c * GROUP + CHUNK:2 * lo + (c + 1) * GROUP, cs] = q_dec[cr]
        kdt_ref[rs, cs] = k_dec.T.astype(BF16)
        attn_ref[rs, cs] = attn.astype(BF16)
        gl_ref[2 * g:2 * g + 1, cs] = jnp.exp(gl0)
        gl_ref[2 * g + 1:2 * g + 2, cs] = jnp.exp(gl1)


def _dn_scan_kernel(u_ref, wq_ref, kdt_ref, attn_ref, gl_ref, z_ref, nw_ref, o_ref, s_ref,
                    *, tt, hg):
    @pl.when(pl.program_id(2) == 0)
    def _():
        s_ref[...] = jnp.zeros_like(s_ref)

    nw = nw_ref[...]
    zeros = jnp.zeros((CHUNK, HEAD_DIM), F32)
    cols = [slice(hh * HEAD_DIM, (hh + 1) * HEAD_DIM) for hh in range(hg)]
    s_mats = [s_ref[hh] for hh in range(hg)]
    for g in range(tt // GROUP):
        lo = g * GROUP
        rs = slice(lo, lo + GROUP)
        v_prev = [zeros] * hg
        outs = [[] for _ in range(hg)]
        for c in range(2):
            r = [_dot(wq_ref[2 * lo + c * GROUP:2 * lo + (c + 1) * GROUP, cs], s_mats[hh])
                 for hh, cs in enumerate(cols)]
            v_new = [u_ref[lo + c * CHUNK:lo + (c + 1) * CHUNK, cs] - r[hh][0:CHUNK]
                     for hh, cs in enumerate(cols)]
            for hh, cs in enumerate(cols):
                v_upd = jnp.concatenate([zeros, v_new[hh]] if c else [v_new[hh], zeros], axis=0)
                s_mats[hh] = (s_mats[hh] * gl_ref[2 * g + c:2 * g + c + 1, cs]
                              + _dot(kdt_ref[rs, cs], v_upd))
            for hh, cs in enumerate(cols):
                v_att = jnp.concatenate([v_prev[hh], v_new[hh]] if c else [v_new[hh], zeros], axis=0)
                outs[hh].append(r[hh][CHUNK:GROUP]
                                + _dot(attn_ref[lo + c * CHUNK:lo + (c + 1) * CHUNK, cs], v_att))
            v_prev = v_new
        for hh, cs in enumerate(cols):
            o = jnp.concatenate(outs[hh], axis=0)
            ms = jnp.mean(o * o, axis=-1, keepdims=True)
            z = z_ref[rs, cs]
            o_ref[rs, cs] = (((o * lax.rsqrt(ms + EPS)) * nw) * (z * _sigmoid(z))).astype(o_ref.dtype)
    for hh in range(hg):
        s_ref[hh] = s_mats[hh]


def _deltanet(p, gates, bsz, seq, d_lru, d_dn, conv_w3, layer, norm_w, tt=512, hg_prep=4,
              hg_scan=16):
    nheads = d_dn // HEAD_DIM
    m = bsz * seq
    tt = min(tt, seq)
    hg_prep = min(hg_prep, nheads)
    hg_scan = min(hg_scan, nheads)
    nt = seq // tt
    base = 2 * d_lru // HEAD_DIM
    ncb = 2 * tt // GROUP

    def prep_cols(off):
        return pl.BlockSpec((tt, hg_prep * HEAD_DIM),
                            lambda b, h, t: (b * nt + t, (base + off * nheads) // hg_prep + h))

    def prep_cw(off):
        return pl.BlockSpec((None, CONV_W, hg_prep * HEAD_DIM),
                            lambda b, h, t: (layer, 0, off * nheads // hg_prep + h))

    assert base % hg_prep == 0 and nheads % hg_prep == 0 and nheads % hg_scan == 0
    wide = hg_prep * HEAD_DIM
    row_blk = lambda rows: pl.BlockSpec((rows, wide), lambda b, h, t: (b * nt + t, h))
    u, wq, kdt, attn, gl = pl.pallas_call(
        functools.partial(_dn_prep_kernel, tt=tt, hg=hg_prep, nheads=nheads),
        grid=(bsz, nheads // hg_prep, nt),
        in_specs=[prep_cols(0), prep_cols(1), prep_cols(2),
                  pl.BlockSpec((tt, LANES), lambda b, h, t: (b * nt + t, 0)),
                  prep_cw(0), prep_cw(1), prep_cw(2)],
        out_specs=[row_blk(tt), row_blk(2 * tt), row_blk(tt), row_blk(tt), row_blk(ncb)],
        out_shape=[jax.ShapeDtypeStruct((m, d_dn), F32),
                   jax.ShapeDtypeStruct((2 * m, d_dn), BF16),
                   jax.ShapeDtypeStruct((m, d_dn), BF16),
                   jax.ShapeDtypeStruct((m, d_dn), BF16),
                   jax.ShapeDtypeStruct((m // CHUNK, d_dn), F32)],
        scratch_shapes=[pltpu.VMEM((HIST, wide), F32)] * 3,
        compiler_params=_params(("arbitrary", "arbitrary", "arbitrary")),
        name="dn_prep",
    )(p, p, p, gates, conv_w3, conv_w3, conv_w3)

    wide = hg_scan * HEAD_DIM
    row_blk = lambda rows: pl.BlockSpec((rows, wide), lambda b, h, t: (b * nt + t, h))
    z_off = (base + 3 * nheads) // hg_scan
    assert (base + 3 * nheads) % hg_scan == 0
    return pl.pallas_call(
        functools.partial(_dn_scan_kernel, tt=tt, hg=hg_scan),
        grid=(bsz, nheads // hg_scan, nt),
        in_specs=[row_blk(tt), row_blk(2 * tt), row_blk(tt), row_blk(tt), row_blk(ncb),
                  pl.BlockSpec((tt, wide), lambda b, h, t: (b * nt + t, z_off + h)),
                  pl.BlockSpec((1, HEAD_DIM), lambda b, h, t: (0, 0))],
        out_specs=row_blk(tt),
        out_shape=jax.ShapeDtypeStruct((m, d_dn), BF16),
        scratch_shapes=[pltpu.VMEM((hg_scan, HEAD_DIM, HEAD_DIM), F32)],
        compiler_params=_params(("arbitrary", "arbitrary", "arbitrary")),
        name="dn_scan",
    )(u, wq, kdt, attn, gl, p, norm_w.reshape(1, HEAD_DIM))


def _stream_kernel(*refs, n_a, n_w, w_is_nk, shortconv, has_res, mi, nj, nb, tm, tn, ks, layer,
                   blocks_per_seq):
    a_refs, w_hbm = refs[:n_a], refs[n_a]
    rest = list(refs[n_a + 1:])
    cw_ref = rest.pop(0) if shortconv else None
    r_ref = rest.pop(0) if has_res else None
    o_ref, wbf0, wbf1, stage, sem = rest[:5]
    hist = rest[5] if shortconv else None
    wbf = (wbf0, wbf1)
    j = pl.program_id(0)
    i = pl.program_id(1)
    lin = j * mi + i
    n_steps = nj * mi

    def slab_copies(l, slot):
        jb = lax.rem(lax.div(l, mi), nj)
        r = lax.rem(l, mi)
        out = []
        for n in range(n_w):
            if w_is_nk:
                src = w_hbm.at[layer, pl.ds(jb * tn + r * ks, ks), :]
            else:
                src = w_hbm.at[layer, pl.ds(r * ks, ks), pl.ds((n * nb + jb) * tn, tn)]
            out.append(pltpu.make_async_copy(src, stage.at[slot, n], sem.at[slot, n]))
        return out

    def cast_slab(slot, r, dst):
        rows = pl.ds(pl.multiple_of(r * ks, ks), ks)
        for n in range(n_w):
            cols = slice(None) if w_is_nk else slice(n * tn, (n + 1) * tn)
            dst[rows, cols] = stage[slot, n].astype(BF16)

    @pl.when(lin == 0)
    def _():
        for c in slab_copies(0, 0):
            c.start()
        for r in range(mi):
            for c in slab_copies(r + 1, (r + 1) % 2):
                c.start()
            for c in slab_copies(r, r % 2):
                c.wait()
            cast_slab(r % 2, r, wbf[0])

    def body(cur, oth):
        slot = lax.rem(i, 2)
        for c in slab_copies(lin + mi, slot):
            c.wait()
        for c in slab_copies(lin + mi + 1, 1 - slot):
            c.start()
        cast_slab(slot, i, oth)
        proj = None
        off = 0
        for a_ref in a_refs:
            kk = a_ref.shape[1]
            if w_is_nk:
                part = lax.dot_general(a_ref[...], cur[:, off:off + kk], _NT,
                                       preferred_element_type=F32)
            else:
                part = jnp.dot(a_ref[...], cur[off:off + kk, :], preferred_element_type=F32)
            proj = part if proj is None else proj + part
            off += kk
        if has_res:
            proj = proj + r_ref[...]
        if shortconv:
            @pl.when(lax.rem(i, blocks_per_seq) == 0)
            def _():
                hist[...] = jnp.zeros_like(hist)

            xin, gate_b, gate_c, z = (proj[:, n * tn:(n + 1) * tn] for n in range(4))
            conv = _causal_conv(gate_c * xin, hist, cw_ref[...], SC_CONV_W, tm, False)
            proj = (gate_b * conv) * (z * _sigmoid(z))
        o_ref[...] = proj.astype(o_ref.dtype)

    @pl.when(lax.rem(j, 2) == 0)
    def _():
        body(wbf[0], wbf[1])

    @pl.when(lax.rem(j, 2) == 1)
    def _():
        body(wbf[1], wbf[0])

    @pl.when(lin == n_steps - 1)
    def _():
        for c in slab_copies(lin + mi + 1, 1 - lax.rem(i, 2)):
            c.wait()


def _stream_matmul(a_parts, w3, layer, n_cols, *, w_is_nk=False, conv_w3=None, seq=None, res=None,
                   out_dtype=F32, tm=1024, tn=1024, name="stream_matmul"):
    m = a_parts[0].shape[0]
    k = sum(a.shape[1] for a in a_parts)
    shortconv = conv_w3 is not None
    n_w = 4 if shortconv else 1
    tm = min(tm, seq if shortconv else m)
    if (m // tm) % 2:
        tm //= 2
    tn = max(t for t in range(LANES, min(tn, n_cols) + 1, LANES) if n_cols % t == 0)
    mi, nj = m // tm, n_cols // tn
    assert m % tm == 0 and mi % 2 == 0 and not (shortconv and w_is_nk)
    if w_is_nk:
        ks = tn // mi
        wbf_shape, stage_shape = (tn, k), (2, 1, ks, k)
        assert tn % mi == 0 and ks % 16 == 0
    else:
        ks = k // mi
        wbf_shape, stage_shape = (k, n_w * tn), (2, n_w, ks, tn)
        assert k % mi == 0 and ks % 16 == 0
    in_specs = [pl.BlockSpec((tm, a.shape[1]), lambda j, i: (i, 0)) for a in a_parts]
    in_specs.append(pl.BlockSpec(memory_space=pl.ANY))
    args = list(a_parts) + [w3]
    scratch = [pltpu.VMEM(wbf_shape, BF16), pltpu.VMEM(wbf_shape, BF16),
               pltpu.VMEM(stage_shape, F32), pltpu.SemaphoreType.DMA((2, n_w))]
    if shortconv:
        in_specs.append(pl.BlockSpec((None, SC_CONV_W, tn), lambda j, i: (layer, 0, j)))
        args.append(conv_w3)
        scratch.append(pltpu.VMEM((HIST, tn), F32))
    if res is not None:
        in_specs.append(pl.BlockSpec((tm, tn), lambda j, i: (i, j)))
        args.append(res)
    return pl.pallas_call(
        functools.partial(_stream_kernel, n_a=len(a_parts), n_w=n_w, w_is_nk=w_is_nk,
                          shortconv=shortconv, has_res=res is not None, mi=mi,
                          nj=nj, nb=nj, tm=tm, tn=tn, ks=ks, layer=layer,
                          blocks_per_seq=(seq // tm) if shortconv else 1),
        grid=(nj, mi),
        in_specs=in_specs,
        out_specs=pl.BlockSpec((tm, tn), lambda j, i: (i, j)),
        out_shape=jax.ShapeDtypeStruct((m, n_cols), out_dtype),
        scratch_shapes=scratch,
        compiler_params=_params(("arbitrary", "arbitrary"),
                                VMEM_LIMIT_RES if res is not None else VMEM_LIMIT),
        name=name,
    )(*args)


def kernel(x, even_norm_w, even_w_in, lru_conv_w, lru_conv_b, lru_w_r, lru_b_r, lru_w_i, lru_b_i,
           lru_lambda, dn_conv_w, dn_a_log, dn_dt_bias, dn_norm_w, even_w_out, odd_norm_w,
           odd_w_in, odd_conv_w, odd_w_out, final_norm_w):
    bsz, seq, d = x.shape
    x2 = x.reshape(bsz * seq, d)
    d_lru = lru_lambda.shape[1]
    d_dn = dn_conv_w.shape[2] // 3
    nheads = d_dn // HEAD_DIM
    n_main = 2 * d_lru + 4 * d_dn
    depth = even_norm_w.shape[0] + odd_norm_w.shape[0]
    for layer in range(depth):
        j = layer // 2
        if layer % 2 == 0:
            w_in_nk = jnp.swapaxes(even_w_in, 1, 2)
            w_tail = jnp.pad(w_in_nk[j, n_main:, :], ((0, LANES - 2 * nheads), (0, 0)))
            h, gates = _norm_gates(x2, even_norm_w[j], w_tail, dn_a_log[j], dn_dt_bias[j], nheads)
            p = _stream_matmul([h], w_in_nk, j, n_main, w_is_nk=True, name="in_proj0")
            ya = _rglru(p, bsz, seq, d_lru, lru_conv_w[j], lru_conv_b[j], lru_w_r[j], lru_b_r[j],
                        lru_w_i[j], lru_b_i[j], lru_lambda[j])
            yb = _deltanet(p, gates, bsz, seq, d_lru, d_dn, dn_conv_w, j, dn_norm_w[j])
            x2 = _stream_matmul([ya, yb], even_w_out, j, d, res=x2, name="out_proj0")
        else:
            h = _rmsnorm(x2, odd_norm_w[j], BF16)
            y = _stream_matmul([h], odd_w_in, j, odd_conv_w.shape[2], conv_w3=odd_conv_w, seq=seq,
                               out_dtype=BF16, tn=SC_TN, name="shortconv")
            x2 = _stream_matmul([y], odd_w_out, j, d, res=x2, name="out_proj1")
    return _rmsnorm(x2, final_norm_w, x.dtype).reshape(bsz, seq, d)
```
